```python
import jax, jax.numpy as jnp
from jax import lax
import numpy as np

D_MODEL = 2048
BATCH = 4
SEQ = 4096
DEPTH = 2

CHUNK = 64
Q_BLOCK = 128
SB_HEADS = 8
SB_HEAD_DIM = 128
SB_WIDTH = SB_HEADS * SB_HEAD_DIM
MLA_HEADS = 8
MLA_NOPE = 128
MLA_ROPE = 64
MLA_V = 128
MLA_Q_LORA = 768
MLA_KV_LORA = 512
ROPE_THETA = 10000.0
D_FF = 5632
CONV_WIDTH = 3
EPS = 1e-6
IN_SPLITS = [SB_WIDTH, SB_WIDTH, SB_WIDTH, MLA_Q_LORA, MLA_KV_LORA, MLA_ROPE, D_MODEL, D_MODEL]
IN_COLS = int(sum(IN_SPLITS))

kernel_name = "hybrid_stickbreak_mla_convffn"


def rms_norm(x, g):
    xf = x.astype(jnp.float32)
    y = xf * lax.rsqrt(jnp.mean(xf * xf, axis=-1, keepdims=True) + EPS)
    return (y * g.astype(jnp.float32)).astype(x.dtype)


def rope_tables(seq, dim, dtype):
    inv = 1.0 / (ROPE_THETA ** (jnp.arange(0, dim, 2, dtype=jnp.float32) / dim))
    ang = jnp.arange(seq, dtype=jnp.float32)[:, None] * inv[None, :]
    return jnp.cos(ang).astype(dtype), jnp.sin(ang).astype(dtype)


def apply_rope(x, cos, sin):
    x1, x2 = jnp.split(x, 2, axis=-1)
    return jnp.concatenate([x1 * cos - x2 * sin, x1 * sin + x2 * cos], axis=-1)


def to_heads(t, n_heads):
    b, s, _ = t.shape
    return t.reshape(b, s, n_heads, -1).transpose(0, 2, 1, 3)


def from_heads(t):
    b, h, s, d = t.shape
    return t.transpose(0, 2, 1, 3).reshape(b, s, h * d)


def stick_breaking_attention(q, k, v):
    s_len = q.shape[2]
    scale = q.shape[-1] ** -0.5
    outs = []
    for blk in range(s_len // Q_BLOCK):
        q0 = blk * Q_BLOCK
        end = q0 + Q_BLOCK
        z = jnp.einsum('bhqd,bhkd->bhqk', q[:, :, q0:end], k[:, :, :end]).astype(jnp.float32) * scale
        qpos = q0 + jnp.arange(Q_BLOCK)[:, None]
        kpos = jnp.arange(end)[None, :]
        mask = kpos < qpos
        log_fail = jnp.where(mask, jax.nn.log_sigmoid(-z), 0.0)
        suffix = lax.cumsum(log_fail, axis=3, reverse=True) - log_fail
        a = jnp.where(mask, jnp.exp(jax.nn.log_sigmoid(z) + suffix), 0.0)
        outs.append(jnp.einsum('bhqk,bhkd->bhqd', a.astype(v.dtype), v[:, :, :end]))
    return jnp.concatenate(outs, axis=2)


def latent_attention(q_nope, q_rope, k_nope, k_rope, v):
    s_len = q_nope.shape[2]
    scale = (MLA_NOPE + MLA_ROPE) ** -0.5
    outs = []
    for blk in range(s_len // Q_BLOCK):
        q0 = blk * Q_BLOCK
        end = q0 + Q_BLOCK
        s_nope = jnp.einsum('bhqd,bhkd->bhqk', q_nope[:, :, q0:end], k_nope[:, :, :end])
        s_rope = jnp.einsum('bhqd,bkd->bhqk', q_rope[:, :, q0:end], k_rope[:, :end])
        s = (s_nope.astype(jnp.float32) + s_rope.astype(jnp.float32)) * scale
        qchunk = (q0 + jnp.arange(Q_BLOCK))[:, None] // CHUNK
        kchunk = jnp.arange(end)[None, :] // CHUNK
        s = jnp.where(kchunk <= qchunk, s, -jnp.inf)
        p = jax.nn.softmax(s, axis=-1)
        outs.append(jnp.einsum('bhqk,bhkd->bhqd', p.astype(v.dtype), v[:, :, :end]))
    return jnp.concatenate(outs, axis=2)


def causal_depthwise_conv(u, w, b):
    s_len = u.shape[1]
    up = jnp.pad(u, ((0, 0), (CONV_WIDTH - 1, 0), (0, 0)))
    acc = b + w[0] * up[:, 0:s_len]
    for i in range(1, CONV_WIDTH):
        acc = acc + w[i] * up[:, i:i + s_len]
    return acc


def hybrid_layer(x, cos, sin, norm1_g, w_in, b_gate, q_norm_g, w_uq, kv_norm_g, w_ukv,
                 w_proj_sb, w_proj_mla, w_out, norm2_g, w_up, conv_w, conv_b, w_down):
    b, s, _ = x.shape
    h = rms_norm(x, norm1_g)
    proj = h @ w_in
    cuts = [int(c) for c in np.cumsum(IN_SPLITS)[:-1]]
    sb_q, sb_k, sb_v, c_q, c_kv, k_rope, g_sb, g_mla = jnp.split(proj, cuts, axis=-1)

    o_sb = stick_breaking_attention(to_heads(sb_q, SB_HEADS), to_heads(sb_k, SB_HEADS),
                                    to_heads(sb_v, SB_HEADS))
    o_sb = from_heads(o_sb)

    q = to_heads(rms_norm(c_q, q_norm_g) @ w_uq, MLA_HEADS)
    q_nope, q_rope = q[..., :MLA_NOPE], apply_rope(q[..., MLA_NOPE:], cos, sin)
    kv = to_heads(rms_norm(c_kv, kv_norm_g) @ w_ukv, MLA_HEADS)
    k_nope, v = kv[..., :MLA_NOPE], kv[..., MLA_NOPE:]
    k_rope = apply_rope(k_rope, cos, sin)
    o_mla = from_heads(latent_attention(q_nope, q_rope, k_nope, k_rope, v))

    gate_sb = jax.nn.sigmoid(g_sb + b_gate[:D_MODEL])
    gate_mla = jax.nn.sigmoid(g_mla + b_gate[D_MODEL:])
    mixed = gate_sb * (o_sb @ w_proj_sb) + gate_mla * (o_mla @ w_proj_mla)
    x = x + mixed @ w_out

    h = rms_norm(x, norm2_g)
    u = causal_depthwise_conv(h @ w_up, conv_w, conv_b)
    gate, val = jnp.split(u, 2, axis=-1)
    x = x + (jax.nn.silu(gate) * val) @ w_down
    return x


def setup_inputs(seed: int = 0) -> dict:
    key = jax.random.key(seed)
    ks = jax.random.split(key, 20)
    f32 = jnp.float32

    def w(k, shape, fan_in):
        return jax.random.normal(k, shape, f32) * (fan_in ** -0.5)

    def gain(k, shape):
        return 1.0 + 0.02 * jax.random.normal(k, shape, f32)

    return {
        "x": jax.random.normal(ks[0], (BATCH, SEQ, D_MODEL), f32),
        "norm1_g": gain(ks[1], (DEPTH, D_MODEL)),
        "w_in": w(ks[2], (DEPTH, D_MODEL, IN_COLS), D_MODEL),
        "b_gate": 0.02 * jax.random.normal(ks[3], (DEPTH, 2 * D_MODEL), f32),
        "q_norm_g": gain(ks[4], (DEPTH, MLA_Q_LORA)),
        "w_uq": w(ks[5], (DEPTH, MLA_Q_LORA, MLA_HEADS * (MLA_NOPE + MLA_ROPE)), MLA_Q_LORA),
        "kv_norm_g": gain(ks[6], (DEPTH, MLA_KV_LORA)),
        "w_ukv": w(ks[7], (DEPTH, MLA_KV_LORA, MLA_HEADS * (MLA_NOPE + MLA_V)), MLA_KV_LORA),
        "w_proj_sb": w(ks[8], (DEPTH, SB_WIDTH, D_MODEL), SB_WIDTH),
        "w_proj_mla": w(ks[9], (DEPTH, MLA_HEADS * MLA_V, D_MODEL), MLA_HEADS * MLA_V),
        "w_out": w(ks[10], (DEPTH, D_MODEL, D_MODEL), D_MODEL),
        "norm2_g": gain(ks[11], (DEPTH, D_MODEL)),
        "w_up": w(ks[12], (DEPTH, D_MODEL, 2 * D_FF), D_MODEL),
        "conv_w": w(ks[13], (DEPTH, CONV_WIDTH, 2 * D_FF), CONV_WIDTH),
        "conv_b": 0.02 * jax.random.normal(ks[14], (DEPTH, 2 * D_FF), f32),
        "w_down": w(ks[15], (DEPTH, D_FF, D_MODEL), D_FF),
        "final_g": gain(ks[16], (D_MODEL,)),
    }


def reference(x, norm1_g, w_in, b_gate, q_norm_g, w_uq, kv_norm_g, w_ukv, w_proj_sb,
              w_proj_mla, w_out, norm2_g, w_up, conv_w, conv_b, w_down, final_g):
    cos, sin = rope_tables(x.shape[1], MLA_ROPE, x.dtype)
    for l in range(DEPTH):
        x = hybrid_layer(x, cos, sin, norm1_g[l], w_in[l], b_gate[l], q_norm_g[l], w_uq[l],
                         kv_norm_g[l], w_ukv[l], w_proj_sb[l], w_proj_mla[l], w_out[l],
                         norm2_g[l], w_up[l], conv_w[l], conv_b[l], w_down[l])
    return rms_norm(x, final_g)
```

```python
import functools

import jax
import jax.numpy as jnp
from jax import lax
from jax.experimental import pallas as pl
from jax.experimental.pallas import tpu as pltpu

F32 = jnp.float32
BF16 = jnp.bfloat16

EPS = 1e-6
ROPE_THETA = 10000.0
CHUNK = 64
SB_HEADS = 8
SB_HEAD_DIM = 128
MLA_HEADS = 8
MLA_NOPE = 128
MLA_ROPE = 64
MLA_V = 128
MLA_Q_LORA = 768
MLA_KV_LORA = 512
CONV_WIDTH = 3

LANES = 128
SUBLANES = 8
MLA_QK_PAD = 2 * LANES
VMEM_LIMIT_BYTES = 56 * 1024 * 1024

_SBW = SB_HEADS * SB_HEAD_DIM
_IN_SBQ = 0
_IN_SBK = _SBW
_IN_SBV = 2 * _SBW
_IN_GSB = 3 * _SBW
_IN_BLOCK = 512


def _params(n_axes):
    return pltpu.CompilerParams(
        dimension_semantics=("arbitrary",) * n_axes, vmem_limit_bytes=VMEM_LIMIT_BYTES)


def _rms(x, g):
    return x * lax.rsqrt(jnp.mean(x * x, axis=-1, keepdims=True) + EPS) * g


def _rope_group(g, c, s1, s2):
    return g * c + pltpu.roll(g, 96, 1) * s1 + pltpu.roll(g, 32, 1) * s2


def _dot(a, b):
    return jnp.dot(a, b, preferred_element_type=F32)


def _dot_nt(a, b):
    return lax.dot_general(a, b, (((1,), (1,)), ((), ())), preferred_element_type=F32)


def _rmsnorm_kernel(x_ref, g_ref, o_ref):
    o_ref[...] = _rms(x_ref[...], g_ref[...]).astype(o_ref.dtype)


def _rmsnorm(x, g, bm):
    t, d = x.shape
    return pl.pallas_call(
        _rmsnorm_kernel,
        out_shape=jax.ShapeDtypeStruct((t, d), BF16),
        grid=(t // bm,),
        in_specs=[pl.BlockSpec((bm, d), lambda i: (i, 0)),
                  pl.BlockSpec((1, d), lambda i: (0, 0))],
        out_specs=pl.BlockSpec((bm, d), lambda i: (i, 0)),
        compiler_params=_params(1),
        name="rmsnorm",
    )(x, g.reshape(1, d))


def _matmul_kernel(a_ref, w_ref, o_ref):
    o_ref[...] = _dot(a_ref[...], w_ref[...]).astype(o_ref.dtype)


def _matmul(a, w, bm, bn):
    t, k = a.shape
    n = w.shape[1]
    return pl.pallas_call(
        _matmul_kernel,
        out_shape=jax.ShapeDtypeStruct((t, n), BF16),
        grid=(t // bm, n // bn),
        in_specs=[pl.BlockSpec((bm, k), lambda i, j: (i, 0)),
                  pl.BlockSpec((k, bn), lambda i, j: (0, j))],
        out_specs=pl.BlockSpec((bm, bn), lambda i, j: (i, j)),
        compiler_params=_params(2),
        name="in_proj",
    )(a, w)


def _mla_q_kernel(cq_ref, g_ref, w_ref, c_ref, s1_ref, s2_ref, o_ref, *, scale):
    h = _rms(cq_ref[...].astype(F32), g_ref[...]).astype(BF16)
    y = _dot(h, w_ref[...])
    c, s1, s2 = c_ref[...], s1_ref[...], s2_ref[...]
    for hd in range(MLA_HEADS):
        lo = hd * MLA_QK_PAD
        o_ref[:, lo:lo + LANES] = (y[:, lo:lo + LANES] * scale).astype(o_ref.dtype)
        rot = _rope_group(y[:, lo + LANES:lo + MLA_QK_PAD], c, s1, s2)
        o_ref[:, lo + LANES:lo + MLA_QK_PAD] = (rot * scale).astype(o_ref.dtype)


def _mla_q(proj, cq_col, g, w, tabs, bm, seq, scale):
    t = proj.shape[0]
    n = w.shape[1]
    nsb = seq // bm
    tab_spec = pl.BlockSpec((bm, LANES), lambda i: (i % nsb, 0))
    return pl.pallas_call(
        functools.partial(_mla_q_kernel, scale=scale),
        out_shape=jax.ShapeDtypeStruct((t, n), BF16),
        grid=(t // bm,),
        in_specs=[pl.BlockSpec((bm, MLA_Q_LORA), lambda i: (i, cq_col // MLA_Q_LORA)),
                  pl.BlockSpec((1, MLA_Q_LORA), lambda i: (0, 0)),
                  pl.BlockSpec((MLA_Q_LORA, n), lambda i: (0, 0)),
                  tab_spec, tab_spec, tab_spec],
        out_specs=pl.BlockSpec((bm, n), lambda i: (i, 0)),
        compiler_params=_params(1),
        name="mla_q",
    )(proj, g.reshape(1, MLA_Q_LORA), w, *tabs)


def _mla_kv_kernel(ckv_ref, kr_ref, g_ref, w_ref, c_ref, s1_ref, s2_ref, k_ref, v_ref):
    h = _rms(ckv_ref[...].astype(F32), g_ref[...]).astype(BF16)
    y = _dot(h, w_ref[...])
    rot = _rope_group(kr_ref[...].astype(F32), c_ref[...], s1_ref[...], s2_ref[...])
    rot = rot.astype(k_ref.dtype)
    kw = MLA_HEADS * MLA_NOPE
    for hd in range(MLA_HEADS):
        lo = hd * MLA_QK_PAD
        k_ref[:, lo:lo + LANES] = y[:, hd * MLA_NOPE:(hd + 1) * MLA_NOPE].astype(k_ref.dtype)
        k_ref[:, lo + LANES:lo + MLA_QK_PAD] = rot
    v_ref[...] = y[:, kw:].astype(v_ref.dtype)


def _mla_kv(proj, ckv_col, kr_col, g, w, tabs, bm, seq):
    t = proj.shape[0]
    nsb = seq // bm
    tab_spec = pl.BlockSpec((bm, LANES), lambda i: (i % nsb, 0))
    return pl.pallas_call(
        _mla_kv_kernel,
        out_shape=(jax.ShapeDtypeStruct((t, MLA_HEADS * MLA_QK_PAD), BF16),
                   jax.ShapeDtypeStruct((t, MLA_HEADS * MLA_V), BF16)),
        grid=(t // bm,),
        in_specs=[pl.BlockSpec((bm, MLA_KV_LORA), lambda i: (i, ckv_col // MLA_KV_LORA)),
                  pl.BlockSpec((bm, LANES), lambda i: (i, kr_col // LANES)),
                  pl.BlockSpec((1, MLA_KV_LORA), lambda i: (0, 0)),
                  pl.BlockSpec(w.shape, lambda i: (0, 0)),
                  tab_spec, tab_spec, tab_spec],
        out_specs=(pl.BlockSpec((bm, MLA_HEADS * MLA_QK_PAD), lambda i: (i, 0)),
                   pl.BlockSpec((bm, MLA_HEADS * MLA_V), lambda i: (i, 0))),
        compiler_params=_params(1),
        name="mla_kv",
    )(proj, proj, g.reshape(1, MLA_KV_LORA), w, *tabs)


def _sb_attn_kernel(q_ref, k_ref, v_ref, o_ref, *, blk, scale):
    i = pl.program_id(2)
    q = q_ref[...]
    row = lax.broadcasted_iota(jnp.int32, (blk, blk), 0)
    col = lax.broadcasted_iota(jnp.int32, (blk, blk), 1)
    later = (row > col).astype(BF16)
    causal = col < row

    def tile(kb, carry, diagonal):
        run, acc = carry
        start = pl.multiple_of(kb * blk, blk)
        k = k_ref[pl.ds(start, blk), :]
        v = v_ref[pl.ds(start, blk), :]
        z = _dot_nt(q, k) * scale
        softplus = jnp.maximum(z, 0.0) + jnp.log1p(jnp.exp(-jnp.abs(z)))
        log_fail = -softplus
        if diagonal:
            log_fail = jnp.where(causal, log_fail, 0.0)
        hi = log_fail.astype(BF16)
        lo = (log_fail - hi.astype(F32)).astype(BF16)
        suffix = _dot(hi, later) + _dot(lo, later)
        a = jnp.exp((z - softplus) + suffix + run)
        if diagonal:
            a = jnp.where(causal, a, 0.0)
        acc = acc + _dot(a.astype(BF16), v)
        run = run + jnp.sum(log_fail, axis=1, keepdims=True)
        return run, acc

    carry = (jnp.zeros((blk, 1), F32), jnp.zeros((blk, SB_HEAD_DIM), F32))
    carry = tile(i, carry, True)
    carry = lax.fori_loop(0, i, lambda t, c: tile(i - 1 - t, c, False), carry)
    o_ref[...] = carry[1].astype(o_ref.dtype)


def _sb_attn(proj, batch, seq, blk):
    t = proj.shape[0]
    nq = seq // blk
    qc, kc, vc = (c // SB_HEAD_DIM for c in (_IN_SBQ, _IN_SBK, _IN_SBV))
    return pl.pallas_call(
        functools.partial(_sb_attn_kernel, blk=blk, scale=SB_HEAD_DIM ** -0.5),
        out_shape=jax.ShapeDtypeStruct((t, _SBW), BF16),
        grid=(batch, SB_HEADS, nq),
        in_specs=[pl.BlockSpec((blk, SB_HEAD_DIM), lambda b, h, i: (b * nq + i, qc + h)),
                  pl.BlockSpec((seq, SB_HEAD_DIM), lambda b, h, i: (b, kc + h)),
                  pl.BlockSpec((seq, SB_HEAD_DIM), lambda b, h, i: (b, vc + h))],
        out_specs=pl.BlockSpec((blk, SB_HEAD_DIM), lambda b, h, i: (b * nq + i, h)),
        compiler_params=_params(3),
        name="sb_attn",
    )(proj, proj, proj)


def _mla_attn_kernel(q_ref, k_ref, v_ref, o_ref, *, blk):
    i = pl.program_id(2)
    q = q_ref[...]
    qchunk = lax.broadcasted_iota(jnp.int32, (blk, blk), 0) // CHUNK
    kchunk = lax.broadcasted_iota(jnp.int32, (blk, blk), 1) // CHUNK
    visible = kchunk <= qchunk

    def tile(kb, carry, diagonal):
        m, l, acc = carry
        start = pl.multiple_of(kb * blk, blk)
        k = k_ref[pl.ds(start, blk), :]
        v = v_ref[pl.ds(start, blk), :]
        s = _dot_nt(q, k)
        if diagonal:
            s = jnp.where(visible, s, -jnp.inf)
        m_new = jnp.maximum(m, jnp.max(s, axis=1, keepdims=True))
        alpha = jnp.exp(m - m_new)
        p = jnp.exp(s - m_new)
        l = alpha * l + jnp.sum(p, axis=1, keepdims=True)
        acc = alpha * acc + _dot(p.astype(BF16), v)
        return m_new, l, acc

    carry = (jnp.full((blk, 1), -jnp.inf, F32), jnp.zeros((blk, 1), F32),
             jnp.zeros((blk, MLA_V), F32))
    carry = tile(i, carry, True)
    carry = lax.fori_loop(0, i, lambda t, c: tile(i - 1 - t, c, False), carry)
    o_ref[...] = (carry[2] / carry[1]).astype(o_ref.dtype)


def _mla_attn(q, k, v, batch, seq, blk):
    t = q.shape[0]
    nq = seq // blk
    return pl.pallas_call(
        functools.partial(_mla_attn_kernel, blk=blk),
        out_shape=jax.ShapeDtypeStruct((t, MLA_HEADS * MLA_V), BF16),
        grid=(batch, MLA_HEADS, nq),
        in_specs=[pl.BlockSpec((blk, MLA_QK_PAD), lambda b, h, i: (b * nq + i, h)),
                  pl.BlockSpec((seq, MLA_QK_PAD), lambda b, h, i: (b, h)),
                  pl.BlockSpec((seq, MLA_V), lambda b, h, i: (b, h))],
        out_specs=pl.BlockSpec((blk, MLA_V), lambda b, h, i: (b * nq + i, h)),
        compiler_params=_params(3),
        name="mla_attn",
    )(q, k, v)


def _merge_kernel(osb_ref, omla_ref, wsb_ref, wmla_ref, gsb_ref, gmla_ref, bsb_ref, bmla_ref,
                  o_ref):
    gate_sb = jax.nn.sigmoid(gsb_ref[...].astype(F32) + bsb_ref[...])
    gate_mla = jax.nn.sigmoid(gmla_ref[...].astype(F32) + bmla_ref[...])
    mixed = (gate_sb * _dot(osb_ref[...], wsb_ref[...])
             + gate_mla * _dot(omla_ref[...], wmla_ref[...]))
    o_ref[...] = mixed.astype(o_ref.dtype)


def _merge(o_sb, o_mla, w_sb, w_mla, proj, gsb_col, gmla_col, b_gate, bm, bn):
    t, k = o_sb.shape
    d = w_sb.shape[1]
    nj = d // bn
    return pl.pallas_call(
        _merge_kernel,
        out_shape=jax.ShapeDtypeStruct((t, d), BF16),
        grid=(t // bm, nj),
        in_specs=[pl.BlockSpec((bm, k), lambda i, j: (i, 0)),
                  pl.BlockSpec((bm, k), lambda i, j: (i, 0)),
                  pl.BlockSpec((k, bn), lambda i, j: (0, j)),
                  pl.BlockSpec((k, bn), lambda i, j: (0, j)),
                  pl.BlockSpec((bm, bn), lambda i, j: (i, gsb_col // bn + j)),
                  pl.BlockSpec((bm, bn), lambda i, j: (i, gmla_col // bn + j)),
                  pl.BlockSpec((1, bn), lambda i, j: (0, j)),
                  pl.BlockSpec((1, bn), lambda i, j: (0, nj + j))],
        out_specs=pl.BlockSpec((bm, bn), lambda i, j: (i, j)),
        compiler_params=_params(2),
        name="merge",
    )(o_sb, o_mla, w_sb, w_mla, proj, proj, b_gate, b_gate)


def _residual_norm_kernel(a_ref, w_ref, x_ref, g_ref, o_ref, h_ref):
    k = pl.program_id(1)
    part = _dot(a_ref[...], w_ref[...])

    @pl.when(k == 0)
    def _():
        o_ref[...] = x_ref[...] + part

    @pl.when(k > 0)
    def _():
        o_ref[...] += part

    @pl.when(k == pl.num_programs(1) - 1)
    def _():
        h_ref[...] = _rms(o_ref[...], g_ref[...]).astype(h_ref.dtype)


def _residual_norm(a, w, x, g, bm, bk, norm_dtype):
    t, kdim = a.shape
    d = w.shape[1]
    return pl.pallas_call(
        _residual_norm_kernel,
        out_shape=(jax.ShapeDtypeStruct((t, d), F32), jax.ShapeDtypeStruct((t, d), norm_dtype)),
        grid=(t // bm, kdim // bk),
        in_specs=[pl.BlockSpec((bm, bk), lambda i, k: (i, k)),
                  pl.BlockSpec((bk, d), lambda i, k: (k, 0)),
                  pl.BlockSpec((bm, d), lambda i, k: (i, 0)),
                  pl.BlockSpec((1, d), lambda i, k: (0, 0))],
        out_specs=(pl.BlockSpec((bm, d), lambda i, k: (i, 0)),
                   pl.BlockSpec((bm, d), lambda i, k: (i, 0))),
        compiler_params=_params(2),
        name="residual_norm",
    )(a, w, x, g.reshape(1, d))


def _ffn_up_kernel(h_ref, wg_ref, wv_ref, cwg_ref, cwv_ref, cbg_ref, cbv_ref, o_ref,
                   sg_ref, sv_ref, *, bm, blocks_per_seq):
    i = pl.program_id(1)
    starts_sequence = (i % blocks_per_seq) == 0
    h = h_ref[...]

    def conv(u, s_ref, cw_ref, cb_ref):
        @pl.when(starts_sequence)
        def _():
            s_ref[0:SUBLANES, :] = jnp.zeros((SUBLANES, u.shape[1]), F32)

        @pl.when(jnp.logical_not(starts_sequence))
        def _():
            s_ref[0:SUBLANES, :] = s_ref[bm:bm + SUBLANES, :]

        s_ref[SUBLANES:SUBLANES + bm, :] = u
        acc = cb_ref[...] + cw_ref[0:1, :] * s_ref[SUBLANES - 2:SUBLANES - 2 + bm, :]
        acc = acc + cw_ref[1:2, :] * s_ref[SUBLANES - 1:SUBLANES - 1 + bm, :]
        return acc + cw_ref[2:3, :] * u

    gate = conv(_dot(h, wg_ref[...]), sg_ref, cwg_ref, cbg_ref)
    val = conv(_dot(h, wv_ref[...]), sv_ref, cwv_ref, cbv_ref)
    o_ref[...] = (gate * jax.nn.sigmoid(gate) * val).astype(o_ref.dtype)


def _ffn_up(h, w_up, conv_w, conv_b, seq, bm, bn):
    t, d = h.shape
    d_ff = w_up.shape[1] // 2
    nj = d_ff // bn
    return pl.pallas_call(
        functools.partial(_ffn_up_kernel, bm=bm, blocks_per_seq=seq // bm),
        out_shape=jax.ShapeDtypeStruct((t, d_ff), BF16),
        grid=(nj, t // bm),
        in_specs=[pl.BlockSpec((bm, d), lambda j, i: (i, 0)),
                  pl.BlockSpec((d, bn), lambda j, i: (0, j)),
                  pl.BlockSpec((d, bn), lambda j, i: (0, nj + j)),
                  pl.BlockSpec((CONV_WIDTH, bn), lambda j, i: (0, j)),
                  pl.BlockSpec((CONV_WIDTH, bn), lambda j, i: (0, nj + j)),
                  pl.BlockSpec((1, bn), lambda j, i: (0, j)),
                  pl.BlockSpec((1, bn), lambda j, i: (0, nj + j))],
        out_specs=pl.BlockSpec((bm, bn), lambda j, i: (i, j)),
        scratch_shapes=[pltpu.VMEM((bm + SUBLANES, bn), F32),
                        pltpu.VMEM((bm + SUBLANES, bn), F32)],
        compiler_params=_params(2),
        name="ffn_up",
    )(h, w_up, w_up, conv_w, conv_w, conv_b, conv_b)


def _rope_tables(seq):
    half = MLA_ROPE // 2
    inv = 1.0 / (ROPE_THETA ** (jnp.arange(0, MLA_ROPE, 2, dtype=F32) / MLA_ROPE))
    ang = jnp.arange(seq, dtype=F32)[:, None] * inv[None, :]
    cos, sin = jnp.cos(ang), jnp.sin(ang)
    zero = jnp.zeros((seq, half), F32)
    tail = jnp.zeros((seq, LANES - MLA_ROPE), F32)
    c = jnp.concatenate([cos, cos, tail], axis=1)
    s1 = jnp.concatenate([-sin, zero, tail], axis=1)
    s2 = jnp.concatenate([zero, sin, tail], axis=1)
    return c, s1, s2


def _block(n, want):
    b = min(n, want)
    assert n % b == 0, (n, want)
    return b


def _prep_w_in(w_in, d_model):
    cuts = [0]
    for wdt in (_SBW, _SBW, _SBW, MLA_Q_LORA, MLA_KV_LORA, MLA_ROPE, d_model, d_model):
        cuts.append(cuts[-1] + wdt)
    seg = [w_in[:, cuts[n]:cuts[n + 1]] for n in range(8)]
    sbq, sbk, sbv, cq, ckv, krope, gsb, gmla = seg
    used = cuts[-1] + (LANES - MLA_ROPE)
    total = -(-used // _IN_BLOCK) * _IN_BLOCK
    pad = jnp.zeros((w_in.shape[0], total - cuts[-1]), w_in.dtype)
    cols = {"gsb": 3 * _SBW, "gmla": 3 * _SBW + d_model, "ckv": 3 * _SBW + 2 * d_model}
    cols["cq"] = cols["ckv"] + MLA_KV_LORA
    cols["krope"] = cols["cq"] + MLA_Q_LORA
    assert cols["ckv"] % MLA_KV_LORA == 0 and cols["cq"] % MLA_Q_LORA == 0
    assert cols["krope"] % LANES == 0 and cols["gmla"] % _IN_BLOCK == 0
    w = jnp.concatenate([sbq, sbk, sbv, gsb, gmla, ckv, cq, krope, pad], axis=1)
    return w.astype(BF16), cols


def _prep_w_uq(w_uq):
    per = MLA_NOPE + MLA_ROPE
    w = w_uq.reshape(w_uq.shape[0], MLA_HEADS, per)
    w = jnp.pad(w, ((0, 0), (0, 0), (0, MLA_QK_PAD - per)))
    return w.reshape(w_uq.shape[0], MLA_HEADS * MLA_QK_PAD).astype(BF16)


def _prep_w_ukv(w_ukv):
    w = w_ukv.reshape(w_ukv.shape[0], MLA_HEADS, MLA_NOPE + MLA_V)
    k = w[:, :, :MLA_NOPE].reshape(w_ukv.shape[0], MLA_HEADS * MLA_NOPE)
    v = w[:, :, MLA_NOPE:].reshape(w_ukv.shape[0], MLA_HEADS * MLA_V)
    return jnp.concatenate([k, v], axis=1).astype(BF16)


def kernel(x, norm1_g, w_in, b_gate, q_norm_g, w_uq, kv_norm_g, w_ukv, w_proj_sb, w_proj_mla,
           w_out, norm2_g, w_up, conv_w, conv_b, w_down, final_g):
    batch, seq, d_model = x.shape
    depth = w_in.shape[0]
    t = batch * seq
    tabs = _rope_tables(seq)
    mla_scale = (MLA_NOPE + MLA_ROPE) ** -0.5

    bm_big = _block(seq, 1024)
    bm_mid = _block(seq, 512)
    blk = _block(seq, 256)

    xf = x.reshape(t, d_model)
    h1 = _rmsnorm(xf, norm1_g[0], bm_mid)
    out = None
    for l in range(depth):
        w_in_l, cols = _prep_w_in(w_in[l], d_model)
        proj = _matmul(h1, w_in_l, bm_big, _IN_BLOCK)
        q_mla = _mla_q(proj, cols["cq"], q_norm_g[l], _prep_w_uq(w_uq[l]), tabs, bm_mid, seq,
                       mla_scale)
        k_mla, v_mla = _mla_kv(proj, cols["ckv"], cols["krope"], kv_norm_g[l],
                               _prep_w_ukv(w_ukv[l]), tabs, bm_mid, seq)
        o_sb = _sb_attn(proj, batch, seq, blk)
        o_mla = _mla_attn(q_mla, k_mla, v_mla, batch, seq, blk)
        mixed = _merge(o_sb, o_mla, w_proj_sb[l].astype(BF16), w_proj_mla[l].astype(BF16), proj,
                       cols["gsb"], cols["gmla"], b_gate[l].reshape(1, -1), bm_big, _IN_BLOCK)
        x1, h2 = _residual_norm(mixed, w_out[l].astype(BF16), xf, norm2_g[l], bm_mid, 512, BF16)
        act = _ffn_up(h2, w_up[l].astype(BF16), conv_w[l], conv_b[l].reshape(1, -1), seq,
                      bm_big, 512)
        last = l == depth - 1
        next_g = final_g if last else norm1_g[l + 1]
        xf, normed = _residual_norm(act, w_down[l].astype(BF16), x1, next_g, bm_mid, 512,
                                    F32 if last else BF16)
        if last:
            out = normed
        else:
            h1 = normed
    return out.reshape(batch, seq, d_model)
```

```python
import functools
import math

import jax
import jax.numpy as jnp
from jax import lax
from jax.experimental import pallas as pl
from jax.experimental.pallas import tpu as pltpu

F32 = jnp.float32
BF16 = jnp.bfloat16

EPS = 1e-6
LOG2E = math.log2(math.e)
ROPE_THETA = 10000.0
CHUNK = 64
SB_HEADS = 8
SB_HEAD_DIM = 128
MLA_HEADS = 8
MLA_NOPE = 128
MLA_ROPE = 64
MLA_V = 128
MLA_Q_LORA = 768
MLA_KV_LORA = 512
CONV_WIDTH = 3

LANES = 128
SUBLANES = 8
MLA_QK_PAD = 2 * LANES
VMEM_LIMIT_BYTES = 56 * 1024 * 1024

_SBW = SB_HEADS * SB_HEAD_DIM
_IN_SBQ = 0
_IN_SBK = _SBW
_IN_SBV = 2 * _SBW
_IN_GSB = 3 * _SBW
_IN_BLOCK = 512


def _params(n_axes):
    return pltpu.CompilerParams(
        dimension_semantics=("arbitrary",) * n_axes, vmem_limit_bytes=VMEM_LIMIT_BYTES)


def _rms(x, g):
    return x * lax.rsqrt(jnp.mean(x * x, axis=-1, keepdims=True) + EPS) * g


def _rope_group(g, c, s1, s2):
    return g * c + pltpu.roll(g, 96, 1) * s1 + pltpu.roll(g, 32, 1) * s2


def _dot(a, b):
    return jnp.dot(a, b, preferred_element_type=F32)


def _dot_nt(a, b):
    return lax.dot_general(a, b, (((1,), (1,)), ((), ())), preferred_element_type=F32)


def _rmsnorm_kernel(x_ref, g_ref, o_ref):
    o_ref[...] = _rms(x_ref[...], g_ref[...]).astype(o_ref.dtype)


def _rmsnorm(x, g, bm):
    t, d = x.shape
    return pl.pallas_call(
        _rmsnorm_kernel,
        out_shape=jax.ShapeDtypeStruct((t, d), x.dtype),
        grid=(t // bm,),
        in_specs=[pl.BlockSpec((bm, d), lambda i: (i, 0)),
                  pl.BlockSpec((1, d), lambda i: (0, 0))],
        out_specs=pl.BlockSpec((bm, d), lambda i: (i, 0)),
        compiler_params=_params(1),
        name="rmsnorm",
    )(x, g.reshape(1, d))


def _norm_matmul_kernel(x_ref, g_ref, w_ref, o_ref, h_ref):
    @pl.when(pl.program_id(1) == 0)
    def _():
        h_ref[...] = _rms(x_ref[...], g_ref[...]).astype(h_ref.dtype)

    o_ref[...] = _dot(h_ref[...], w_ref[...]).astype(o_ref.dtype)


def _norm_matmul(x, g, w, bm, bn):
    t, k = x.shape
    n = w.shape[1]
    return pl.pallas_call(
        _norm_matmul_kernel,
        out_shape=jax.ShapeDtypeStruct((t, n), BF16),
        grid=(t // bm, n // bn),
        in_specs=[pl.BlockSpec((bm, k), lambda i, j: (i, 0)),
                  pl.BlockSpec((1, k), lambda i, j: (0, 0)),
                  pl.BlockSpec((k, bn), lambda i, j: (0, j))],
        out_specs=pl.BlockSpec((bm, bn), lambda i, j: (i, j)),
        scratch_shapes=[pltpu.VMEM((bm, k), BF16)],
        compiler_params=_params(2),
        name="in_proj",
    )(x, g.reshape(1, k), w)


def _mla_q_kernel(cq_ref, g_ref, w_ref, c_ref, s1_ref, s2_ref, o_ref):
    h = _rms(cq_ref[...].astype(F32), g_ref[...]).astype(BF16)
    y = _dot(h, w_ref[...])
    c, s1, s2 = c_ref[...], s1_ref[...], s2_ref[...]
    for hd in range(MLA_HEADS):
        lo = hd * MLA_QK_PAD
        o_ref[:, lo:lo + LANES] = y[:, lo:lo + LANES].astype(o_ref.dtype)
        rot = _rope_group(y[:, lo + LANES:lo + MLA_QK_PAD], c, s1, s2)
        o_ref[:, lo + LANES:lo + MLA_QK_PAD] = rot.astype(o_ref.dtype)


def _mla_q(proj, cq_col, g, w, tabs, bm, seq):
    t = proj.shape[0]
    n = w.shape[1]
    nsb = seq // bm
    tab_spec = pl.BlockSpec((bm, LANES), lambda i: (i % nsb, 0))
    return pl.pallas_call(
        _mla_q_kernel,
        out_shape=jax.ShapeDtypeStruct((t, n), BF16),
        grid=(t // bm,),
        in_specs=[pl.BlockSpec((bm, MLA_Q_LORA), lambda i: (i, cq_col // MLA_Q_LORA)),
                  pl.BlockSpec((1, MLA_Q_LORA), lambda i: (0, 0)),
                  pl.BlockSpec((MLA_Q_LORA, n), lambda i: (0, 0)),
                  tab_spec, tab_spec, tab_spec],
        out_specs=pl.BlockSpec((bm, n), lambda i: (i, 0)),
        compiler_params=_params(1),
        name="mla_q",
    )(proj, g.reshape(1, MLA_Q_LORA), w, *tabs)


def _mla_kv_kernel(ckv_ref, kr_ref, g_ref, w_ref, c_ref, s1_ref, s2_ref, k_ref, v_ref):
    h = _rms(ckv_ref[...].astype(F32), g_ref[...]).astype(BF16)
    y = _dot(h, w_ref[...])
    rot = _rope_group(kr_ref[...].astype(F32), c_ref[...], s1_ref[...], s2_ref[...])
    rot = rot.astype(k_ref.dtype)
    kw = MLA_HEADS * MLA_NOPE
    for hd in range(MLA_HEADS):
        lo = hd * MLA_QK_PAD
        k_ref[:, lo:lo + LANES] = y[:, hd * MLA_NOPE:(hd + 1) * MLA_NOPE].astype(k_ref.dtype)
        k_ref[:, lo + LANES:lo + MLA_QK_PAD] = rot
    v_ref[...] = y[:, kw:].astype(v_ref.dtype)


def _mla_kv(proj, ckv_col, kr_col, g, w, tabs, bm, seq):
    t = proj.shape[0]
    nsb = seq // bm
    tab_spec = pl.BlockSpec((bm, LANES), lambda i: (i % nsb, 0))
    return pl.pallas_call(
        _mla_kv_kernel,
        out_shape=(jax.ShapeDtypeStruct((t, MLA_HEADS * MLA_QK_PAD), BF16),
                   jax.ShapeDtypeStruct((t, MLA_HEADS * MLA_V), BF16)),
        grid=(t // bm,),
        in_specs=[pl.BlockSpec((bm, MLA_KV_LORA), lambda i: (i, ckv_col // MLA_KV_LORA)),
                  pl.BlockSpec((bm, LANES), lambda i: (i, kr_col // LANES)),
                  pl.BlockSpec((1, MLA_KV_LORA), lambda i: (0, 0)),
                  pl.BlockSpec(w.shape, lambda i: (0, 0)),
                  tab_spec, tab_spec, tab_spec],
        out_specs=(pl.BlockSpec((bm, MLA_HEADS * MLA_QK_PAD), lambda i: (i, 0)),
                   pl.BlockSpec((bm, MLA_HEADS * MLA_V), lambda i: (i, 0))),
        compiler_params=_params(1),
        name="mla_kv",
    )(proj, proj, g.reshape(1, MLA_KV_LORA), w, *tabs)


def _sb_attn_kernel(q_ref, k_ref, v_ref, o_ref, run_ref, acc_ref, *, bq, bk):
    i = pl.program_id(2)
    n_diag = bq // bk
    row = lax.broadcasted_iota(jnp.int32, (bk, bk), 0)
    col = lax.broadcasted_iota(jnp.int32, (bk, bk), 1)
    later = (row > col).astype(BF16)
    later2 = jnp.concatenate([later, later], axis=0)

    run_ref[...] = jnp.zeros(run_ref.shape, F32)
    acc_ref[...] = jnp.zeros(acc_ref.shape, F32)

    def tile(kb, diag):
        start = pl.multiple_of(kb * bk, bk)
        k = k_ref[pl.ds(start, bk), :]
        v = v_ref[pl.ds(start, bk), :]
        z = _dot_nt(q_ref[...], k)
        neg_abs = lax.bitcast_convert_type(
            lax.bitcast_convert_type(z, jnp.uint32) | jnp.uint32(0x80000000), F32)
        sp = jnp.maximum(z, 0.0) + jnp.log(1.0 + jnp.exp2(neg_abs)) * LOG2E
        if diag is not None:
            qpos = lax.broadcasted_iota(jnp.int32, (bq, bk), 0)
            kpos = lax.broadcasted_iota(jnp.int32, (bq, bk), 1) + diag * bk
            causal = kpos < qpos
            sp = jnp.where(causal, sp, 0.0)
        hi = lax.bitcast_convert_type(
            lax.bitcast_convert_type(sp, jnp.uint32) & jnp.uint32(0xFFFF0000), F32)
        split = jnp.concatenate([hi.astype(BF16), (sp - hi).astype(BF16)], axis=1)
        suffix = _dot(split, later2)
        a = jnp.exp2(z - sp - suffix)
        if diag is not None:
            a = jnp.where(causal, a, 0.0)
        run = run_ref[...]
        acc_ref[...] += jnp.exp2(-run) * _dot(a.astype(BF16), v)
        run_ref[...] = run + (suffix[:, 0:1] + sp[:, 0:1])

    for d in range(n_diag - 1, -1, -1):
        tile(i * n_diag + d, d)
    n_before = i * n_diag

    def body(t, carry):
        tile(n_before - 1 - t, None)
        return carry

    lax.fori_loop(0, n_before, body, 0)
    o_ref[...] = acc_ref[...].astype(o_ref.dtype)


def _sb_attn(proj, batch, seq, bq, bk):
    t = proj.shape[0]
    nq = seq // bq
    qc, kc, vc = (c // SB_HEAD_DIM for c in (_IN_SBQ, _IN_SBK, _IN_SBV))
    return pl.pallas_call(
        functools.partial(_sb_attn_kernel, bq=bq, bk=bk),
        out_shape=jax.ShapeDtypeStruct((t, _SBW), BF16),
        grid=(batch, SB_HEADS, nq),
        in_specs=[pl.BlockSpec((bq, SB_HEAD_DIM), lambda b, h, i: (b * nq + i, qc + h)),
                  pl.BlockSpec((seq, SB_HEAD_DIM), lambda b, h, i: (b, kc + h)),
                  pl.BlockSpec((seq, SB_HEAD_DIM), lambda b, h, i: (b, vc + h))],
        out_specs=pl.BlockSpec((bq, SB_HEAD_DIM), lambda b, h, i: (b * nq + i, h)),
        scratch_shapes=[pltpu.VMEM((bq, LANES), F32), pltpu.VMEM((bq, SB_HEAD_DIM), F32)],
        compiler_params=_params(3),
        name="sb_attn",
    )(proj, proj, proj)


def _mla_attn_kernel(q_ref, k_ref, v_ref, o_ref, m_ref, acc_ref, *, bq, bk, sub):
    i = pl.program_id(2)
    n_diag = bq // bk
    m_ref[...] = jnp.full(m_ref.shape, -jnp.inf, F32)
    acc_ref[...] = jnp.zeros(acc_ref.shape, F32)
    ones = jnp.ones((bk, LANES), BF16)

    def tile(kb, diag):
        start = pl.multiple_of(kb * bk, bk)
        k = k_ref[pl.ds(start, bk), :]
        v1 = jnp.concatenate([v_ref[pl.ds(start, bk), :], ones], axis=1)
        for s in range(bq // sub):
            rows = slice(s * sub, (s + 1) * sub)
            if diag is not None and (s + 1) * sub <= diag * bk:
                continue
            sc = _dot_nt(q_ref[rows, :], k)
            if diag is not None and s * sub < (diag + 1) * bk:
                qchunk = (lax.broadcasted_iota(jnp.int32, (sub, bk), 0) + s * sub) // CHUNK
                kchunk = (lax.broadcasted_iota(jnp.int32, (sub, bk), 1) + diag * bk) // CHUNK
                sc = jnp.where(kchunk <= qchunk, sc, -jnp.inf)
            m = m_ref[rows, :]
            m_new = jnp.maximum(m, jnp.max(sc, axis=1, keepdims=True))
            alpha = jnp.exp2(m - m_new)
            p = jnp.exp2(sc - jnp.tile(m_new, (1, bk // LANES)))
            acc_ref[rows, :] = (jnp.tile(alpha, (1, 2)) * acc_ref[rows, :]
                                + _dot(p.astype(BF16), v1))
            m_ref[rows, :] = m_new

    for d in range(n_diag):
        tile(i * n_diag + d, d)
    n_before = i * n_diag

    def body(t, carry):
        tile(n_before - 1 - t, None)
        return carry

    lax.fori_loop(0, n_before, body, 0)
    o_ref[...] = (acc_ref[:, 0:MLA_V] / acc_ref[:, MLA_V:2 * MLA_V]).astype(o_ref.dtype)


def _mla_attn(q, k, v, batch, seq, bq, bk, sub):
    t = q.shape[0]
    nq = seq // bq
    return pl.pallas_call(
        functools.partial(_mla_attn_kernel, bq=bq, bk=bk, sub=sub),
        out_shape=jax.ShapeDtypeStruct((t, MLA_HEADS * MLA_V), BF16),
        grid=(batch, MLA_HEADS, nq),
        in_specs=[pl.BlockSpec((bq, MLA_QK_PAD), lambda b, h, i: (b * nq + i, h)),
                  pl.BlockSpec((seq, MLA_QK_PAD), lambda b, h, i: (b, h)),
                  pl.BlockSpec((seq, MLA_V), lambda b, h, i: (b, h))],
        out_specs=pl.BlockSpec((bq, MLA_V), lambda b, h, i: (b * nq + i, h)),
        scratch_shapes=[pltpu.VMEM((bq, LANES), F32), pltpu.VMEM((bq, 2 * MLA_V), F32)],
        compiler_params=_params(3),
        name="mla_attn",
    )(q, k, v)


def _merge_kernel(osb_ref, omla_ref, wsb_ref, wmla_ref, gsb_ref, gmla_ref, bsb_ref, bmla_ref,
                  o_ref):
    gate_sb = jax.nn.sigmoid(gsb_ref[...].astype(F32) + bsb_ref[...])
    gate_mla = jax.nn.sigmoid(gmla_ref[...].astype(F32) + bmla_ref[...])
    mixed = (gate_sb * _dot(osb_ref[...], wsb_ref[...])
             + gate_mla * _dot(omla_ref[...], wmla_ref[...]))
    o_ref[...] = mixed.astype(o_ref.dtype)


def _merge(o_sb, o_mla, w_sb, w_mla, proj, gsb_col, gmla_col, b_gate, bm, bn):
    t, k = o_sb.shape
    d = w_sb.shape[1]
    nj = d // bn
    return pl.pallas_call(
        _merge_kernel,
        out_shape=jax.ShapeDtypeStruct((t, d), BF16),
        grid=(t // bm, nj),
        in_specs=[pl.BlockSpec((bm, k), lambda i, j: (i, 0)),
                  pl.BlockSpec((bm, k), lambda i, j: (i, 0)),
                  pl.BlockSpec((k, bn), lambda i, j: (0, j)),
                  pl.BlockSpec((k, bn), lambda i, j: (0, j)),
                  pl.BlockSpec((bm, bn), lambda i, j: (i, gsb_col // bn + j)),
                  pl.BlockSpec((bm, bn), lambda i, j: (i, gmla_col // bn + j)),
                  pl.BlockSpec((1, bn), lambda i, j: (0, j)),
                  pl.BlockSpec((1, bn), lambda i, j: (0, nj + j))],
        out_specs=pl.BlockSpec((bm, bn), lambda i, j: (i, j)),
        compiler_params=_params(2),
        name="merge",
    )(o_sb, o_mla, w_sb, w_mla, proj, proj, b_gate, b_gate)


def _residual_norm_kernel(a_ref, w_ref, x_ref, g_ref, o_ref, h_ref):
    out = x_ref[...] + _dot(a_ref[...], w_ref[...])
    o_ref[...] = out
    h_ref[...] = _rms(out, g_ref[...]).astype(h_ref.dtype)


def _residual_norm(a, w, x, g, bm):
    t, k = a.shape
    d = w.shape[1]
    return pl.pallas_call(
        _residual_norm_kernel,
        out_shape=(jax.ShapeDtypeStruct((t, d), F32), jax.ShapeDtypeStruct((t, d), BF16)),
        grid=(t // bm,),
        in_specs=[pl.BlockSpec((bm, k), lambda i: (i, 0)),
                  pl.BlockSpec((k, d), lambda i: (0, 0)),
                  pl.BlockSpec((bm, d), lambda i: (i, 0)),
                  pl.BlockSpec((1, d), lambda i: (0, 0))],
        out_specs=(pl.BlockSpec((bm, d), lambda i: (i, 0)),
                   pl.BlockSpec((bm, d), lambda i: (i, 0))),
        compiler_params=_params(1),
        name="residual_norm",
    )(a, w, x, g.reshape(1, d))


def _residual_kernel(a_ref, w_ref, x_ref, o_ref):
    o_ref[...] = x_ref[...] + _dot(a_ref[...], w_ref[...])


def _residual(a, w, x, bm, bn):
    t, k = a.shape
    d = w.shape[1]
    return pl.pallas_call(
        _residual_kernel,
        out_shape=jax.ShapeDtypeStruct((t, d), F32),
        grid=(d // bn, t // bm),
        in_specs=[pl.BlockSpec((bm, k), lambda j, i: (i, 0)),
                  pl.BlockSpec((k, bn), lambda j, i: (0, j)),
                  pl.BlockSpec((bm, bn), lambda j, i: (i, j))],
        out_specs=pl.BlockSpec((bm, bn), lambda j, i: (i, j)),
        compiler_params=_params(2),
        name="ffn_down",
    )(a, w, x)


def _ffn_up_kernel(h_ref, wg_ref, wv_ref, cwg_ref, cwv_ref, cbg_ref, cbv_ref, o_ref,
                   sg_ref, sv_ref, *, bm, chunk, blocks_per_seq):
    i = pl.program_id(1)
    starts_sequence = (i % blocks_per_seq) == 0
    for s_ref in (sg_ref, sv_ref):
        s_ref[0:SUBLANES, :] = jnp.where(starts_sequence, 0.0, s_ref[bm:bm + SUBLANES, :])

    def conv(s_ref, cw_ref, cb_ref, lo):
        acc = cb_ref[...] + cw_ref[0:1, :] * s_ref[lo - 2:lo - 2 + chunk, :]
        acc = acc + cw_ref[1:2, :] * s_ref[lo - 1:lo - 1 + chunk, :]
        return acc + cw_ref[2:3, :] * s_ref[lo:lo + chunk, :]

    for c in range(bm // chunk):
        h = h_ref[c * chunk:(c + 1) * chunk, :]
        lo = SUBLANES + c * chunk
        sg_ref[lo:lo + chunk, :] = _dot(h, wg_ref[...])
        sv_ref[lo:lo + chunk, :] = _dot(h, wv_ref[...])
        gate = conv(sg_ref, cwg_ref, cbg_ref, lo)
        val = conv(sv_ref, cwv_ref, cbv_ref, lo)
        o_ref[c * chunk:(c + 1) * chunk, :] = (
            gate * jax.nn.sigmoid(gate) * val).astype(o_ref.dtype)


def _ffn_up(h, w_up, conv_w, conv_b, seq, bm, bn, chunk):
    t, d = h.shape
    d_ff = w_up.shape[1] // 2
    nj = d_ff // bn
    return pl.pallas_call(
        functools.partial(_ffn_up_kernel, bm=bm, chunk=chunk, blocks_per_seq=seq // bm),
        out_shape=jax.ShapeDtypeStruct((t, d_ff), BF16),
        grid=(nj, t // bm),
        in_specs=[pl.BlockSpec((bm, d), lambda j, i: (i, 0)),
                  pl.BlockSpec((d, bn), lambda j, i: (0, j)),
                  pl.BlockSpec((d, bn), lambda j, i: (0, nj + j)),
                  pl.BlockSpec((CONV_WIDTH, bn), lambda j, i: (0, j)),
                  pl.BlockSpec((CONV_WIDTH, bn), lambda j, i: (0, nj + j)),
                  pl.BlockSpec((1, bn), lambda j, i: (0, j)),
                  pl.BlockSpec((1, bn), lambda j, i: (0, nj + j))],
        out_specs=pl.BlockSpec((bm, bn), lambda j, i: (i, j)),
        scratch_shapes=[pltpu.VMEM((bm + SUBLANES, bn), F32),
                        pltpu.VMEM((bm + SUBLANES, bn), F32)],
        compiler_params=_params(2),
        name="ffn_up",
    )(h, w_up, w_up, conv_w, conv_w, conv_b, conv_b)


def _rope_tables(seq):
    half = MLA_ROPE // 2
    inv = 1.0 / (ROPE_THETA ** (jnp.arange(0, MLA_ROPE, 2, dtype=F32) / MLA_ROPE))
    ang = jnp.arange(seq, dtype=F32)[:, None] * inv[None, :]
    cos, sin = jnp.cos(ang), jnp.sin(ang)
    zero = jnp.zeros((seq, half), F32)
    tail = jnp.zeros((seq, LANES - MLA_ROPE), F32)
    c = jnp.concatenate([cos, cos, tail], axis=1)
    s1 = jnp.concatenate([-sin, zero, tail], axis=1)
    s2 = jnp.concatenate([zero, sin, tail], axis=1)
    return c, s1, s2


def _block(n, want):
    b = min(n, want)
    assert n % b == 0, (n, want)
    return b


def _prep_w_in(w_in, d_model):
    cuts = [0]
    for wdt in (_SBW, _SBW, _SBW, MLA_Q_LORA, MLA_KV_LORA, MLA_ROPE, d_model, d_model):
        cuts.append(cuts[-1] + wdt)
    seg = [w_in[:, cuts[n]:cuts[n + 1]] for n in range(8)]
    sbq, sbk, sbv, cq, ckv, krope, gsb, gmla = seg
    used = cuts[-1] + (LANES - MLA_ROPE)
    total = -(-used // _IN_BLOCK) * _IN_BLOCK
    pad = jnp.zeros((w_in.shape[0], total - cuts[-1]), w_in.dtype)
    cols = {"gsb": 3 * _SBW, "gmla": 3 * _SBW + d_model, "ckv": 3 * _SBW + 2 * d_model}
    cols["cq"] = cols["ckv"] + MLA_KV_LORA
    cols["krope"] = cols["cq"] + MLA_Q_LORA
    assert cols["ckv"] % MLA_KV_LORA == 0 and cols["cq"] % MLA_Q_LORA == 0
    assert cols["krope"] % LANES == 0 and cols["gmla"] % _IN_BLOCK == 0
    sbq = sbq * (SB_HEAD_DIM ** -0.5 * LOG2E)
    w = jnp.concatenate([sbq, sbk, sbv, gsb, gmla, ckv, cq, krope, pad], axis=1)
    return w.astype(BF16), cols


def _prep_w_uq(w_uq):
    per = MLA_NOPE + MLA_ROPE
    w = w_uq.reshape(w_uq.shape[0], MLA_HEADS, per) * (per ** -0.5 * LOG2E)
    w = jnp.pad(w, ((0, 0), (0, 0), (0, MLA_QK_PAD - per)))
    return w.reshape(w_uq.shape[0], MLA_HEADS * MLA_QK_PAD).astype(BF16)


def _prep_w_ukv(w_ukv):
    w = w_ukv.reshape(w_ukv.shape[0], MLA_HEADS, MLA_NOPE + MLA_V)
    k = w[:, :, :MLA_NOPE].reshape(w_ukv.shape[0], MLA_HEADS * MLA_NOPE)
    v = w[:, :, MLA_NOPE:].reshape(w_ukv.shape[0], MLA_HEADS * MLA_V)
    return jnp.concatenate([k, v], axis=1).astype(BF16)


def kernel(x, norm1_g, w_in, b_gate, q_norm_g, w_uq, kv_norm_g, w_ukv, w_proj_sb, w_proj_mla,
           w_out, norm2_g, w_up, conv_w, conv_b, w_down, final_g):
    batch, seq, d_model = x.shape
    depth = w_in.shape[0]
    t = batch * seq
    tabs = _rope_tables(seq)

    bm_big = _block(seq, 1024)
    bm_mid = _block(seq, 512)
    sb_bq, sb_bk = _block(seq, 1024), _block(seq, 256)
    mla_bq = _block(seq, 2048)
    mla_bk, mla_sub = _block(mla_bq, 1024), _block(mla_bq, 256)

    xf = x.reshape(t, d_model)
    for l in range(depth):
        w_in_l, cols = _prep_w_in(w_in[l], d_model)
        proj = _norm_matmul(xf, norm1_g[l], w_in_l, bm_big, _IN_BLOCK)
        q_mla = _mla_q(proj, cols["cq"], q_norm_g[l], _prep_w_uq(w_uq[l]), tabs, bm_mid, seq)
        k_mla, v_mla = _mla_kv(proj, cols["ckv"], cols["krope"], kv_norm_g[l],
                               _prep_w_ukv(w_ukv[l]), tabs, bm_mid, seq)
        o_sb = _sb_attn(proj, batch, seq, sb_bq, sb_bk)
        o_mla = _mla_attn(q_mla, k_mla, v_mla, batch, seq, mla_bq, mla_bk, mla_sub)
        mixed = _merge(o_sb, o_mla, w_proj_sb[l].astype(BF16), w_proj_mla[l].astype(BF16), proj,
                       cols["gsb"], cols["gmla"], b_gate[l].reshape(1, -1), bm_big, _IN_BLOCK)
        x1, h2 = _residual_norm(mixed, w_out[l].astype(BF16), xf, norm2_g[l], bm_mid)
        act = _ffn_up(h2, w_up[l].astype(BF16), conv_w[l], conv_b[l].reshape(1, -1), seq,
                      bm_big, 512, _block(bm_big, 256))
        xf = _residual(act, w_down[l].astype(BF16), x1, bm_mid, 1024)
    return _rmsnorm(xf, final_g, bm_mid).reshape(batch, seq, d_model)
```

```python
import functools
import math

import jax
import jax.numpy as jnp
from jax import lax
from jax.experimental import pallas as pl
from jax.experimental.pallas import tpu as pltpu

F32 = jnp.float32
BF16 = jnp.bfloat16

EPS = 1e-6
LOG2E = math.log2(math.e)
ROPE_THETA = 10000.0
CHUNK = 64
SB_HEADS = 8
SB_HEAD_DIM = 128
MLA_HEADS = 8
MLA_NOPE = 128
MLA_ROPE = 64
MLA_V = 128
MLA_Q_LORA = 768
MLA_KV_LORA = 512
CONV_WIDTH = 3

LANES = 128
SUBLANES = 8
MLA_QK_PAD = 2 * LANES
VMEM_LIMIT_BYTES = 56 * 1024 * 1024

_SBW = SB_HEADS * SB_HEAD_DIM
_IN_SBQ = 0
_IN_SBK = _SBW
_IN_SBV = 2 * _SBW
_IN_GSB = 3 * _SBW
_IN_BLOCK = 512


def _params(n_axes):
    return pltpu.CompilerParams(
        dimension_semantics=("arbitrary",) * n_axes, vmem_limit_bytes=VMEM_LIMIT_BYTES)


def _rms(x, g):
    return x * lax.rsqrt(jnp.mean(x * x, axis=-1, keepdims=True) + EPS) * g


def _rope_group(g, c, s1, s2):
    return g * c + pltpu.roll(g, 96, 1) * s1 + pltpu.roll(g, 32, 1) * s2


def _dot(a, b):
    return jnp.dot(a, b, preferred_element_type=F32)


def _dot_nt(a, b):
    return lax.dot_general(a, b, (((1,), (1,)), ((), ())), preferred_element_type=F32)


def _rmsnorm_kernel(x_ref, g_ref, o_ref):
    o_ref[...] = _rms(x_ref[...], g_ref[...]).astype(o_ref.dtype)


def _rmsnorm(x, g, bm):
    t, d = x.shape
    return pl.pallas_call(
        _rmsnorm_kernel,
        out_shape=jax.ShapeDtypeStruct((t, d), x.dtype),
        grid=(t // bm,),
        in_specs=[pl.BlockSpec((bm, d), lambda i: (i, 0)),
                  pl.BlockSpec((1, d), lambda i: (0, 0))],
        out_specs=pl.BlockSpec((bm, d), lambda i: (i, 0)),
        compiler_params=_params(1),
        name="rmsnorm",
    )(x, g.reshape(1, d))


def _norm_matmul_kernel(x_ref, g_ref, w_ref, o_ref, h_ref):
    @pl.when(pl.program_id(1) == 0)
    def _():
        h_ref[...] = _rms(x_ref[...], g_ref[...]).astype(h_ref.dtype)

    o_ref[...] = _dot(h_ref[...], w_ref[...]).astype(o_ref.dtype)


def _norm_matmul(x, g, w, layer, bm, bn):
    t, k = x.shape
    n = w.shape[2]
    return pl.pallas_call(
        _norm_matmul_kernel,
        out_shape=jax.ShapeDtypeStruct((t, n), BF16),
        grid=(t // bm, n // bn),
        in_specs=[pl.BlockSpec((bm, k), lambda i, j: (i, 0)),
                  pl.BlockSpec((1, k), lambda i, j: (0, 0)),
                  pl.BlockSpec((None, k, bn), lambda i, j: (layer, 0, j))],
        out_specs=pl.BlockSpec((bm, bn), lambda i, j: (i, j)),
        scratch_shapes=[pltpu.VMEM((bm, k), BF16)],
        compiler_params=_params(2),
        name="in_proj",
    )(x, g.reshape(1, k), w)


def _mla_q_kernel(cq_ref, g_ref, w_ref, c_ref, s1_ref, s2_ref, o_ref):
    h = _rms(cq_ref[...].astype(F32), g_ref[...]).astype(BF16)
    y = _dot(h, w_ref[...])
    c, s1, s2 = c_ref[...], s1_ref[...], s2_ref[...]
    for hd in range(MLA_HEADS):
        lo = hd * MLA_QK_PAD
        o_ref[:, lo:lo + LANES] = y[:, lo:lo + LANES].astype(o_ref.dtype)
        rot = _rope_group(y[:, lo + LANES:lo + MLA_QK_PAD], c, s1, s2)
        o_ref[:, lo + LANES:lo + MLA_QK_PAD] = rot.astype(o_ref.dtype)


def _mla_q(proj, cq_col, g, w, tabs, bm, seq):
    t = proj.shape[0]
    n = w.shape[1]
    nsb = seq // bm
    tab_spec = pl.BlockSpec((bm, LANES), lambda i: (i % nsb, 0))
    return pl.pallas_call(
        _mla_q_kernel,
        out_shape=jax.ShapeDtypeStruct((t, n), BF16),
        grid=(t // bm,),
        in_specs=[pl.BlockSpec((bm, MLA_Q_LORA), lambda i: (i, cq_col // MLA_Q_LORA)),
                  pl.BlockSpec((1, MLA_Q_LORA), lambda i: (0, 0)),
                  pl.BlockSpec((MLA_Q_LORA, n), lambda i: (0, 0)),
                  tab_spec, tab_spec, tab_spec],
        out_specs=pl.BlockSpec((bm, n), lambda i: (i, 0)),
        compiler_params=_params(1),
        name="mla_q",
    )(proj, g.reshape(1, MLA_Q_LORA), w, *tabs)


def _mla_kv_kernel(ckv_ref, kr_ref, g_ref, w_ref, c_ref, s1_ref, s2_ref, k_ref, v_ref):
    h = _rms(ckv_ref[...].astype(F32), g_ref[...]).astype(BF16)
    y = _dot(h, w_ref[...])
    rot = _rope_group(kr_ref[...].astype(F32), c_ref[...], s1_ref[...], s2_ref[...])
    rot = rot.astype(k_ref.dtype)
    kw = MLA_HEADS * MLA_NOPE
    for hd in range(MLA_HEADS):
        lo = hd * MLA_QK_PAD
        k_ref[:, lo:lo + LANES] = y[:, hd * MLA_NOPE:(hd + 1) * MLA_NOPE].astype(k_ref.dtype)
        k_ref[:, lo + LANES:lo + MLA_QK_PAD] = rot
    v_ref[...] = y[:, kw:].astype(v_ref.dtype)


def _mla_kv(proj, ckv_col, kr_col, g, w, tabs, bm, seq):
    t = proj.shape[0]
    nsb = seq // bm
    tab_spec = pl.BlockSpec((bm, LANES), lambda i: (i % nsb, 0))
    return pl.pallas_call(
        _mla_kv_kernel,
        out_shape=(jax.ShapeDtypeStruct((t, MLA_HEADS * MLA_QK_PAD), BF16),
                   jax.ShapeDtypeStruct((t, MLA_HEADS * MLA_V), BF16)),
        grid=(t // bm,),
        in_specs=[pl.BlockSpec((bm, MLA_KV_LORA), lambda i: (i, ckv_col // MLA_KV_LORA)),
                  pl.BlockSpec((bm, LANES), lambda i: (i, kr_col // LANES)),
                  pl.BlockSpec((1, MLA_KV_LORA), lambda i: (0, 0)),
                  pl.BlockSpec(w.shape, lambda i: (0, 0)),
                  tab_spec, tab_spec, tab_spec],
        out_specs=(pl.BlockSpec((bm, MLA_HEADS * MLA_QK_PAD), lambda i: (i, 0)),
                   pl.BlockSpec((bm, MLA_HEADS * MLA_V), lambda i: (i, 0))),
        compiler_params=_params(1),
        name="mla_kv",
    )(proj, proj, g.reshape(1, MLA_KV_LORA), w, *tabs)


def _sb_attn_kernel(q_ref, k_ref, v_ref, o_ref, run_ref, acc_ref, *, bq, bk):
    i = pl.program_id(2)
    n_diag = bq // bk
    row = lax.broadcasted_iota(jnp.int32, (bk, bk), 0)
    col = lax.broadcasted_iota(jnp.int32, (bk, bk), 1)
    later = (row > col).astype(BF16)
    later2 = jnp.concatenate([later, later], axis=0)

    run_ref[...] = jnp.zeros(run_ref.shape, F32)
    acc_ref[...] = jnp.zeros(acc_ref.shape, F32)

    def tile(kb, diag):
        rows = slice(0 if diag is None else diag * bk, bq)
        start = pl.multiple_of(kb * bk, bk)
        k = k_ref[pl.ds(start, bk), :]
        v = v_ref[pl.ds(start, bk), :]
        z = _dot_nt(q_ref[rows, :], k)
        z_pos = jnp.maximum(z, 0.0)
        z_neg = z - z_pos
        log_term = jnp.log(1.0 + jnp.exp2(z_neg - z_pos)) * LOG2E
        sp = z_pos + log_term
        if diag is not None:
            causal = (lax.broadcasted_iota(jnp.int32, z.shape, 1)
                      < lax.broadcasted_iota(jnp.int32, z.shape, 0))
            sp = jnp.where(causal, sp, 0.0)
        hi = sp.astype(BF16)
        split = jnp.concatenate([hi, (sp - hi.astype(F32)).astype(BF16)], axis=1)
        suffix = _dot(split, later2)
        a = jnp.exp2(z_neg - log_term - suffix)
        if diag is not None:
            a = jnp.where(causal, a, 0.0)
        run = run_ref[rows, :]
        acc_ref[rows, :] += jnp.exp2(-run) * _dot(a.astype(BF16), v)
        run_ref[rows, :] = run + (suffix[:, 0:1] + sp[:, 0:1])

    for d in range(n_diag - 1, -1, -1):
        tile(i * n_diag + d, d)
    n_before = i * n_diag

    def body(t, carry):
        tile(n_before - 1 - t, None)
        return carry

    lax.fori_loop(0, n_before, body, 0)
    o_ref[...] = acc_ref[...].astype(o_ref.dtype)


def _sb_attn(proj, batch, seq, bq, bk):
    t = proj.shape[0]
    nq = seq // bq
    qc, kc, vc = (c // SB_HEAD_DIM for c in (_IN_SBQ, _IN_SBK, _IN_SBV))
    return pl.pallas_call(
        functools.partial(_sb_attn_kernel, bq=bq, bk=bk),
        out_shape=jax.ShapeDtypeStruct((t, _SBW), BF16),
        grid=(batch, SB_HEADS, nq),
        in_specs=[pl.BlockSpec((bq, SB_HEAD_DIM), lambda b, h, i: (b * nq + i, qc + h)),
                  pl.BlockSpec((seq, SB_HEAD_DIM), lambda b, h, i: (b, kc + h)),
                  pl.BlockSpec((seq, SB_HEAD_DIM), lambda b, h, i: (b, vc + h))],
        out_specs=pl.BlockSpec((bq, SB_HEAD_DIM), lambda b, h, i: (b * nq + i, h)),
        scratch_shapes=[pltpu.VMEM((bq, LANES), F32), pltpu.VMEM((bq, SB_HEAD_DIM), F32)],
        compiler_params=_params(3),
        name="sb_attn",
    )(proj, proj, proj)


def _mla_attn_kernel(q_ref, k_ref, v_ref, o_ref, m_ref, acc_ref, *, bq, bk, sub):
    i = pl.program_id(2)
    n_diag = bq // bk
    m_ref[...] = jnp.full(m_ref.shape, -jnp.inf, F32)
    acc_ref[...] = jnp.zeros(acc_ref.shape, F32)
    ones = jnp.ones((bk, LANES), BF16)

    def tile(kb, diag):
        start = pl.multiple_of(kb * bk, bk)
        k = k_ref[pl.ds(start, bk), :]
        v1 = jnp.concatenate([v_ref[pl.ds(start, bk), :], ones], axis=1)
        for s in range(bq // sub):
            rows = slice(s * sub, (s + 1) * sub)
            if diag is not None and (s + 1) * sub <= diag * bk:
                continue
            sc = _dot_nt(q_ref[rows, :], k)
            if diag is not None and s * sub < (diag + 1) * bk:
                qchunk = (lax.broadcasted_iota(jnp.int32, (sub, bk), 0) + s * sub) // CHUNK
                kchunk = (lax.broadcasted_iota(jnp.int32, (sub, bk), 1) + diag * bk) // CHUNK
                sc = jnp.where(kchunk <= qchunk, sc, -jnp.inf)
            m = m_ref[rows, :]
            m_new = jnp.maximum(m, jnp.max(sc, axis=1, keepdims=True))
            alpha = jnp.exp2(m - m_new)
            p = jnp.exp2(sc - jnp.tile(m_new, (1, bk // LANES)))
            acc_ref[rows, :] = (jnp.tile(alpha, (1, 2)) * acc_ref[rows, :]
                                + _dot(p.astype(BF16), v1))
            m_ref[rows, :] = m_new

    for d in range(n_diag):
        tile(i * n_diag + d, d)
    n_before = i * n_diag

    def body(t, carry):
        tile(n_before - 1 - t, None)
        return carry

    lax.fori_loop(0, n_before, body, 0)
    o_ref[...] = (acc_ref[:, 0:MLA_V] / acc_ref[:, MLA_V:2 * MLA_V]).astype(o_ref.dtype)


def _mla_attn(q, k, v, batch, seq, bq, bk, sub):
    t = q.shape[0]
    nq = seq // bq
    return pl.pallas_call(
        functools.partial(_mla_attn_kernel, bq=bq, bk=bk, sub=sub),
        out_shape=jax.ShapeDtypeStruct((t, MLA_HEADS * MLA_V), BF16),
        grid=(batch, MLA_HEADS, nq),
        in_specs=[pl.BlockSpec((bq, MLA_QK_PAD), lambda b, h, i: (b * nq + i, h)),
                  pl.BlockSpec((seq, MLA_QK_PAD), lambda b, h, i: (b, h)),
                  pl.BlockSpec((seq, MLA_V), lambda b, h, i: (b, h))],
        out_specs=pl.BlockSpec((bq, MLA_V), lambda b, h, i: (b * nq + i, h)),
        scratch_shapes=[pltpu.VMEM((bq, LANES), F32), pltpu.VMEM((bq, 2 * MLA_V), F32)],
        compiler_params=_params(3),
        name="mla_attn",
    )(q, k, v)


def _merge_kernel(osb_ref, omla_ref, wsb_ref, wmla_ref, gsb_ref, gmla_ref, bsb_ref, bmla_ref,
                  o_ref):
    gate_sb = jax.nn.sigmoid(gsb_ref[...].astype(F32) + bsb_ref[...])
    gate_mla = jax.nn.sigmoid(gmla_ref[...].astype(F32) + bmla_ref[...])
    mixed = (gate_sb * _dot(osb_ref[...], wsb_ref[...])
             + gate_mla * _dot(omla_ref[...], wmla_ref[...]))
    o_ref[...] = mixed.astype(o_ref.dtype)


def _merge(o_sb, o_mla, w_sb, w_mla, layer, proj, gsb_col, gmla_col, b_gate, bm, bn):
    t, k = o_sb.shape
    d = w_sb.shape[2]
    nj = d // bn
    return pl.pallas_call(
        _merge_kernel,
        out_shape=jax.ShapeDtypeStruct((t, d), BF16),
        grid=(t // bm, nj),
        in_specs=[pl.BlockSpec((bm, k), lambda i, j: (i, 0)),
                  pl.BlockSpec((bm, k), lambda i, j: (i, 0)),
                  pl.BlockSpec((None, k, bn), lambda i, j: (layer, 0, j)),
                  pl.BlockSpec((None, k, bn), lambda i, j: (layer, 0, j)),
                  pl.BlockSpec((bm, bn), lambda i, j: (i, gsb_col // bn + j)),
                  pl.BlockSpec((bm, bn), lambda i, j: (i, gmla_col // bn + j)),
                  pl.BlockSpec((1, bn), lambda i, j: (0, j)),
                  pl.BlockSpec((1, bn), lambda i, j: (0, nj + j))],
        out_specs=pl.BlockSpec((bm, bn), lambda i, j: (i, j)),
        compiler_params=_params(2),
        name="merge",
    )(o_sb, o_mla, w_sb, w_mla, proj, proj, b_gate, b_gate)


def _residual_norm_kernel(a_ref, w_ref, x_ref, g_ref, o_ref, h_ref):
    out = x_ref[...] + _dot(a_ref[...], w_ref[...])
    o_ref[...] = out
    h_ref[...] = _rms(out, g_ref[...]).astype(h_ref.dtype)


def _residual_norm(a, w, layer, x, g, bm):
    t, k = a.shape
    d = w.shape[2]
    return pl.pallas_call(
        _residual_norm_kernel,
        out_shape=(jax.ShapeDtypeStruct((t, d), F32), jax.ShapeDtypeStruct((t, d), BF16)),
        grid=(t // bm,),
        in_specs=[pl.BlockSpec((bm, k), lambda i: (i, 0)),
                  pl.BlockSpec((None, k, d), lambda i: (layer, 0, 0)),
                  pl.BlockSpec((bm, d), lambda i: (i, 0)),
                  pl.BlockSpec((1, d), lambda i: (0, 0))],
        out_specs=(pl.BlockSpec((bm, d), lambda i: (i, 0)),
                   pl.BlockSpec((bm, d), lambda i: (i, 0))),
        compiler_params=_params(1),
        name="residual_norm",
    )(a, w, x, g.reshape(1, d))


def _residual_kernel(a_ref, w_ref, x_ref, o_ref):
    o_ref[...] = x_ref[...] + _dot(a_ref[...], w_ref[...])


def _residual(a, w, layer, x, bm, bn):
    t, k = a.shape
    d = w.shape[2]
    return pl.pallas_call(
        _residual_kernel,
        out_shape=jax.ShapeDtypeStruct((t, d), F32),
        grid=(d // bn, t // bm),
        in_specs=[pl.BlockSpec((bm, k), lambda j, i: (i, 0)),
                  pl.BlockSpec((None, k, bn), lambda j, i: (layer, 0, j)),
                  pl.BlockSpec((bm, bn), lambda j, i: (i, j))],
        out_specs=pl.BlockSpec((bm, bn), lambda j, i: (i, j)),
        compiler_params=_params(2),
        name="ffn_down",
    )(a, w, x)


def _ffn_up_kernel(h_ref, wg_ref, wv_ref, cwg_ref, cwv_ref, cbg_ref, cbv_ref, o_ref,
                   sg_ref, sv_ref, wgb_ref, wvb_ref, *, bm, chunk, blocks_per_seq):
    i = pl.program_id(1)

    @pl.when(i == 0)
    def _():
        wgb_ref[...] = wg_ref[...].astype(wgb_ref.dtype)
        wvb_ref[...] = wv_ref[...].astype(wvb_ref.dtype)

    starts_sequence = (i % blocks_per_seq) == 0
    for s_ref in (sg_ref, sv_ref):
        s_ref[0:SUBLANES, :] = jnp.where(starts_sequence, 0.0, s_ref[bm:bm + SUBLANES, :])

    def conv(s_ref, cw_ref, cb_ref, lo):
        acc = cb_ref[...] + cw_ref[0:1, :] * s_ref[lo - 2:lo - 2 + chunk, :]
        acc = acc + cw_ref[1:2, :] * s_ref[lo - 1:lo - 1 + chunk, :]
        return acc + cw_ref[2:3, :] * s_ref[lo:lo + chunk, :]

    for c in range(bm // chunk):
        h = h_ref[c * chunk:(c + 1) * chunk, :]
        lo = SUBLANES + c * chunk
        sg_ref[lo:lo + chunk, :] = _dot(h, wgb_ref[...])
        sv_ref[lo:lo + chunk, :] = _dot(h, wvb_ref[...])
        gate = conv(sg_ref, cwg_ref, cbg_ref, lo)
        val = conv(sv_ref, cwv_ref, cbv_ref, lo)
        o_ref[c * chunk:(c + 1) * chunk, :] = (
            gate * jax.nn.sigmoid(gate) * val).astype(o_ref.dtype)


def _ffn_up(h, w_up, layer, conv_w, conv_b, seq, bm, bn, chunk):
    t, d = h.shape
    d_ff = w_up.shape[2] // 2
    nj = d_ff // bn
    return pl.pallas_call(
        functools.partial(_ffn_up_kernel, bm=bm, chunk=chunk, blocks_per_seq=seq // bm),
        out_shape=jax.ShapeDtypeStruct((t, d_ff), BF16),
        grid=(nj, t // bm),
        in_specs=[pl.BlockSpec((bm, d), lambda j, i: (i, 0)),
                  pl.BlockSpec((None, d, bn), lambda j, i: (layer, 0, j)),
                  pl.BlockSpec((None, d, bn), lambda j, i: (layer, 0, nj + j)),
                  pl.BlockSpec((CONV_WIDTH, bn), lambda j, i: (0, j)),
                  pl.BlockSpec((CONV_WIDTH, bn), lambda j, i: (0, nj + j)),
                  pl.BlockSpec((1, bn), lambda j, i: (0, j)),
                  pl.BlockSpec((1, bn), lambda j, i: (0, nj + j))],
        out_specs=pl.BlockSpec((bm, bn), lambda j, i: (i, j)),
        scratch_shapes=[pltpu.VMEM((bm + SUBLANES, bn), F32),
                        pltpu.VMEM((bm + SUBLANES, bn), F32),
                        pltpu.VMEM((d, bn), BF16), pltpu.VMEM((d, bn), BF16)],
        compiler_params=_params(2),
        name="ffn_up",
    )(h, w_up, w_up, conv_w, conv_w, conv_b, conv_b)


def _rope_tables(seq):
    half = MLA_ROPE // 2
    inv = 1.0 / (ROPE_THETA ** (jnp.arange(0, MLA_ROPE, 2, dtype=F32) / MLA_ROPE))
    ang = jnp.arange(seq, dtype=F32)[:, None] * inv[None, :]
    cos, sin = jnp.cos(ang), jnp.sin(ang)
    zero = jnp.zeros((seq, half), F32)
    tail = jnp.zeros((seq, LANES - MLA_ROPE), F32)
    c = jnp.concatenate([cos, cos, tail], axis=1)
    s1 = jnp.concatenate([-sin, zero, tail], axis=1)
    s2 = jnp.concatenate([zero, sin, tail], axis=1)
    return c, s1, s2


def _block(n, want):
    b = min(n, want)
    assert n % b == 0, (n, want)
    return b


def _in_layout(d_model):
    src = {"sbq": 0, "sbk": _SBW, "sbv": 2 * _SBW, "cq": 3 * _SBW}
    src["ckv"] = src["cq"] + MLA_Q_LORA
    src["krope"] = src["ckv"] + MLA_KV_LORA
    src["gsb"] = src["krope"] + MLA_ROPE
    src["gmla"] = src["gsb"] + d_model
    src["end"] = src["gmla"] + d_model
    dst = {"sbq": _IN_SBQ, "sbk": _IN_SBK, "sbv": _IN_SBV, "gsb": _IN_GSB,
           "gmla": _IN_GSB + d_model}
    dst["ckv"] = dst["gmla"] + d_model
    dst["cq"] = dst["ckv"] + MLA_KV_LORA
    dst["krope"] = dst["cq"] + MLA_Q_LORA
    dst["end"] = -(-(dst["krope"] + LANES) // _IN_BLOCK) * _IN_BLOCK
    assert dst["ckv"] % MLA_KV_LORA == 0 and dst["cq"] % MLA_Q_LORA == 0
    assert dst["krope"] % LANES == 0 and dst["gmla"] % _IN_BLOCK == 0
    return src, dst


def _prep_w_in_kernel(w_ref, o_ref, *, d_model):
    src, dst = _in_layout(d_model)

    def put(name, width, scale=None):
        x = w_ref[:, src[name]:src[name] + width]
        if scale is not None:
            x = x * scale
        o_ref[:, dst[name]:dst[name] + width] = x.astype(o_ref.dtype)

    put("sbq", _SBW, SB_HEAD_DIM ** -0.5 * LOG2E)
    put("sbk", 2 * _SBW)
    put("gsb", 2 * d_model)
    put("ckv", MLA_KV_LORA)
    put("cq", MLA_Q_LORA)
    put("krope", MLA_ROPE)
    tail = dst["krope"] + MLA_ROPE
    o_ref[:, tail:] = jnp.zeros((o_ref.shape[0], dst["end"] - tail), o_ref.dtype)


def _prep_w_in(w_in, rows):
    depth, k, n = w_in.shape
    src, dst = _in_layout(k)
    assert src["end"] == n
    return pl.pallas_call(
        functools.partial(_prep_w_in_kernel, d_model=k),
        out_shape=jax.ShapeDtypeStruct((depth, k, dst["end"]), BF16),
        grid=(depth, k // rows),
        in_specs=[pl.BlockSpec((None, rows, n), lambda l, i: (l, i, 0))],
        out_specs=pl.BlockSpec((None, rows, dst["end"]), lambda l, i: (l, i, 0)),
        compiler_params=_params(2),
        name="prep_w_in",
    )(w_in)


def _cast_kernel(w_ref, o_ref):
    o_ref[...] = w_ref[...].astype(o_ref.dtype)


def _cast_bf16(w, rows):
    depth, k, n = w.shape
    return pl.pallas_call(
        _cast_kernel,
        out_shape=jax.ShapeDtypeStruct(w.shape, BF16),
        grid=(depth, k // rows),
        in_specs=[pl.BlockSpec((None, rows, n), lambda l, i: (l, i, 0))],
        out_specs=pl.BlockSpec((None, rows, n), lambda l, i: (l, i, 0)),
        compiler_params=_params(2),
        name="cast_bf16",
    )(w)


def _prep_w_uq(w_uq):
    per = MLA_NOPE + MLA_ROPE
    w = w_uq.reshape(w_uq.shape[0], MLA_HEADS, per) * (per ** -0.5 * LOG2E)
    w = jnp.pad(w, ((0, 0), (0, 0), (0, MLA_QK_PAD - per)))
    return w.reshape(w_uq.shape[0], MLA_HEADS * MLA_QK_PAD).astype(BF16)


def _prep_w_ukv(w_ukv):
    w = w_ukv.reshape(w_ukv.shape[0], MLA_HEADS, MLA_NOPE + MLA_V)
    k = w[:, :, :MLA_NOPE].reshape(w_ukv.shape[0], MLA_HEADS * MLA_NOPE)
    v = w[:, :, MLA_NOPE:].reshape(w_ukv.shape[0], MLA_HEADS * MLA_V)
    return jnp.concatenate([k, v], axis=1).astype(BF16)


def kernel(x, norm1_g, w_in, b_gate, q_norm_g, w_uq, kv_norm_g, w_ukv, w_proj_sb, w_proj_mla,
           w_out, norm2_g, w_up, conv_w, conv_b, w_down, final_g):
    batch, seq, d_model = x.shape
    depth = w_in.shape[0]
    t = batch * seq
    tabs = _rope_tables(seq)

    bm_big = _block(seq, 1024)
    bm_mid = _block(seq, 512)
    sb_bq, sb_bk = _block(seq, 1024), _block(seq, 256)
    mla_bq = _block(seq, 2048)
    mla_bk, mla_sub = _block(mla_bq, 1024), _block(mla_bq, 256)

    _, cols = _in_layout(d_model)
    w_in_b = _prep_w_in(w_in, 256)
    w_sb_b, w_mla_b = _cast_bf16(w_proj_sb, 512), _cast_bf16(w_proj_mla, 512)
    w_out_b, w_down_b = _cast_bf16(w_out, 512), _cast_bf16(w_down, 512)

    xf = x.reshape(t, d_model)
    for l in range(depth):
        proj = _norm_matmul(xf, norm1_g[l], w_in_b, l, bm_big, _IN_BLOCK)
        q_mla = _mla_q(proj, cols["cq"], q_norm_g[l], _prep_w_uq(w_uq[l]), tabs, bm_mid, seq)
        k_mla, v_mla = _mla_kv(proj, cols["ckv"], cols["krope"], kv_norm_g[l],
                               _prep_w_ukv(w_ukv[l]), tabs, bm_mid, seq)
        o_sb = _sb_attn(proj, batch, seq, sb_bq, sb_bk)
        o_mla = _mla_attn(q_mla, k_mla, v_mla, batch, seq, mla_bq, mla_bk, mla_sub)
        mixed = _merge(o_sb, o_mla, w_sb_b, w_mla_b, l, proj, cols["gsb"], cols["gmla"],
                       b_gate[l].reshape(1, -1), bm_big, _IN_BLOCK)
        x1, h2 = _residual_norm(mixed, w_out_b, l, xf, norm2_g[l], bm_mid)
        act = _ffn_up(h2, w_up, l, conv_w[l], conv_b[l].reshape(1, -1), seq,
                      bm_big, 512, _block(bm_big, 256))
        xf = _residual(act, w_down_b, l, x1, bm_mid, 1024)
    return _rmsnorm(xf, final_g, bm_mid).reshape(batch, seq, d_model)
```

```python
import functools
import math

import jax
import jax.numpy as jnp
from jax import lax
from jax.experimental import pallas as pl
from jax.experimental.pallas import tpu as pltpu

F32 = jnp.float32
BF16 = jnp.bfloat16

EPS = 1e-6
LOG2E = math.log2(math.e)
ROPE_THETA = 10000.0
CHUNK = 64
SB_HEADS = 8
SB_HEAD_DIM = 128
MLA_HEADS = 8
MLA_NOPE = 128
MLA_ROPE = 64
MLA_V = 128
MLA_Q_LORA = 768
MLA_KV_LORA = 512
CONV_WIDTH = 3

LANES = 128
SUBLANES = 8
MLA_QK_PAD = 2 * LANES
VMEM_LIMIT_BYTES = 56 * 1024 * 1024

_SBW = SB_HEADS * SB_HEAD_DIM
_IN_SBQ = 0
_IN_SBK = _SBW
_IN_SBV = 2 * _SBW
_IN_GSB = 3 * _SBW
_IN_BLOCK = 512


def _params(n_axes):
    return pltpu.CompilerParams(
        dimension_semantics=("arbitrary",) * n_axes, vmem_limit_bytes=VMEM_LIMIT_BYTES)


def _rms(x, g):
    return x * lax.rsqrt(jnp.mean(x * x, axis=-1, keepdims=True) + EPS) * g


def _rope_group(g, c, s1, s2):
    return g * c + pltpu.roll(g, 96, 1) * s1 + pltpu.roll(g, 32, 1) * s2


def _dot(a, b):
    return jnp.dot(a, b, preferred_element_type=F32)


def _dot_nt(a, b):
    return lax.dot_general(a, b, (((1,), (1,)), ((), ())), preferred_element_type=F32)


def _rmsnorm_kernel(x_ref, g_ref, o_ref):
    o_ref[...] = _rms(x_ref[...], g_ref[...]).astype(o_ref.dtype)


def _rmsnorm(x, g, bm):
    t, d = x.shape
    return pl.pallas_call(
        _rmsnorm_kernel,
        out_shape=jax.ShapeDtypeStruct((t, d), x.dtype),
        grid=(t // bm,),
        in_specs=[pl.BlockSpec((bm, d), lambda i: (i, 0)),
                  pl.BlockSpec((1, d), lambda i: (0, 0))],
        out_specs=pl.BlockSpec((bm, d), lambda i: (i, 0)),
        compiler_params=_params(1),
        name="rmsnorm",
    )(x, g.reshape(1, d))


def _norm_matmul_kernel(x_ref, g_ref, w_ref, o_ref, h_ref):
    @pl.when(pl.program_id(1) == 0)
    def _():
        h_ref[...] = _rms(x_ref[...], g_ref[...]).astype(h_ref.dtype)

    o_ref[...] = _dot_nt(h_ref[...], w_ref[...]).astype(o_ref.dtype)


def _norm_matmul(x, g, w_t, layer, bm, bn):
    t, k = x.shape
    n = w_t.shape[1]
    return pl.pallas_call(
        _norm_matmul_kernel,
        out_shape=jax.ShapeDtypeStruct((t, n), BF16),
        grid=(t // bm, n // bn),
        in_specs=[pl.BlockSpec((bm, k), lambda i, j: (i, 0)),
                  pl.BlockSpec((1, k), lambda i, j: (0, 0)),
                  pl.BlockSpec((None, bn, k), lambda i, j: (layer, j, 0))],
        out_specs=pl.BlockSpec((bm, bn), lambda i, j: (i, j)),
        scratch_shapes=[pltpu.VMEM((bm, k), BF16)],
        compiler_params=_params(2),
        name="in_proj",
    )(x, g.reshape(1, k), w_t)


def _mla_q_kernel(cq_ref, g_ref, w_ref, c_ref, s1_ref, s2_ref, o_ref):
    h = _rms(cq_ref[...].astype(F32), g_ref[...]).astype(BF16)
    y = _dot(h, w_ref[...])
    c, s1, s2 = c_ref[...], s1_ref[...], s2_ref[...]
    for hd in range(MLA_HEADS):
        lo = hd * MLA_QK_PAD
        o_ref[:, lo:lo + LANES] = y[:, lo:lo + LANES].astype(o_ref.dtype)
        rot = _rope_group(y[:, lo + LANES:lo + MLA_QK_PAD], c, s1, s2)
        o_ref[:, lo + LANES:lo + MLA_QK_PAD] = rot.astype(o_ref.dtype)


def _mla_q(proj, cq_col, g, w, tabs, bm, seq):
    t = proj.shape[0]
    n = w.shape[1]
    nsb = seq // bm
    tab_spec = pl.BlockSpec((bm, LANES), lambda i: (i % nsb, 0))
    return pl.pallas_call(
        _mla_q_kernel,
        out_shape=jax.ShapeDtypeStruct((t, n), BF16),
        grid=(t // bm,),
        in_specs=[pl.BlockSpec((bm, MLA_Q_LORA), lambda i: (i, cq_col // MLA_Q_LORA)),
                  pl.BlockSpec((1, MLA_Q_LORA), lambda i: (0, 0)),
                  pl.BlockSpec((MLA_Q_LORA, n), lambda i: (0, 0)),
                  tab_spec, tab_spec, tab_spec],
        out_specs=pl.BlockSpec((bm, n), lambda i: (i, 0)),
        compiler_params=_params(1),
        name="mla_q",
    )(proj, g.reshape(1, MLA_Q_LORA), w, *tabs)


def _mla_kv_kernel(ckv_ref, kr_ref, g_ref, w_ref, c_ref, s1_ref, s2_ref, k_ref, v_ref):
    h = _rms(ckv_ref[...].astype(F32), g_ref[...]).astype(BF16)
    y = _dot(h, w_ref[...])
    rot = _rope_group(kr_ref[...].astype(F32), c_ref[...], s1_ref[...], s2_ref[...])
    rot = rot.astype(k_ref.dtype)
    kw = MLA_HEADS * MLA_NOPE
    for hd in range(MLA_HEADS):
        lo = hd * MLA_QK_PAD
        k_ref[:, lo:lo + LANES] = y[:, hd * MLA_NOPE:(hd + 1) * MLA_NOPE].astype(k_ref.dtype)
        k_ref[:, lo + LANES:lo + MLA_QK_PAD] = rot
    v_ref[...] = y[:, kw:].astype(v_ref.dtype)


def _mla_kv(proj, ckv_col, kr_col, g, w, tabs, bm, seq):
    t = proj.shape[0]
    nsb = seq // bm
    tab_spec = pl.BlockSpec((bm, LANES), lambda i: (i % nsb, 0))
    return pl.pallas_call(
        _mla_kv_kernel,
        out_shape=(jax.ShapeDtypeStruct((t, MLA_HEADS * MLA_QK_PAD), BF16),
                   jax.ShapeDtypeStruct((t, MLA_HEADS * MLA_V), BF16)),
        grid=(t // bm,),
        in_specs=[pl.BlockSpec((bm, MLA_KV_LORA), lambda i: (i, ckv_col // MLA_KV_LORA)),
                  pl.BlockSpec((bm, LANES), lambda i: (i, kr_col // LANES)),
                  pl.BlockSpec((1, MLA_KV_LORA), lambda i: (0, 0)),
                  pl.BlockSpec(w.shape, lambda i: (0, 0)),
                  tab_spec, tab_spec, tab_spec],
        out_specs=(pl.BlockSpec((bm, MLA_HEADS * MLA_QK_PAD), lambda i: (i, 0)),
                   pl.BlockSpec((bm, MLA_HEADS * MLA_V), lambda i: (i, 0))),
        compiler_params=_params(1),
        name="mla_kv",
    )(proj, proj, g.reshape(1, MLA_KV_LORA), w, *tabs)


def _sb_attn_kernel(q_ref, k_ref, v_ref, o_ref, run_ref, acc_ref, *, bq, bk):
    i = pl.program_id(2)
    n_diag = bq // bk
    row = lax.broadcasted_iota(jnp.int32, (bk, bk), 0)
    col = lax.broadcasted_iota(jnp.int32, (bk, bk), 1)
    later = (row > col).astype(BF16)
    later2 = jnp.concatenate([later, later], axis=0)

    run_ref[...] = jnp.zeros(run_ref.shape, F32)
    acc_ref[...] = jnp.zeros(acc_ref.shape, F32)

    def tile(kb, diag):
        rows = slice(0 if diag is None else diag * bk, bq)
        start = pl.multiple_of(kb * bk, bk)
        k = k_ref[pl.ds(start, bk), :]
        v = v_ref[pl.ds(start, bk), :]
        z = _dot_nt(q_ref[rows, :], k)
        z_pos = jnp.maximum(z, 0.0)
        z_neg = z - z_pos
        log_term = jnp.log(1.0 + jnp.exp2(z_neg - z_pos)) * LOG2E
        sp = z_pos + log_term
        if diag is not None:
            causal = (lax.broadcasted_iota(jnp.int32, z.shape, 1)
                      < lax.broadcasted_iota(jnp.int32, z.shape, 0))
            sp = jnp.where(causal, sp, 0.0)
        hi = sp.astype(BF16)
        split = jnp.concatenate([hi, (sp - hi.astype(F32)).astype(BF16)], axis=1)
        suffix = _dot(split, later2)
        a = jnp.exp2(z_neg - log_term - suffix)
        if diag is not None:
            a = jnp.where(causal, a, 0.0)
        run = run_ref[rows, :]
        acc_ref[rows, :] += jnp.exp2(-run) * _dot(a.astype(BF16), v)
        run_ref[rows, :] = run + (suffix[:, 0:1] + sp[:, 0:1])

    for d in range(n_diag - 1, -1, -1):
        tile(i * n_diag + d, d)
    n_before = i * n_diag

    def body(t, carry):
        tile(n_before - 1 - t, None)
        return carry

    lax.fori_loop(0, n_before, body, 0)
    o_ref[...] = acc_ref[...].astype(o_ref.dtype)


def _sb_attn(proj, batch, seq, bq, bk):
    t = proj.shape[0]
    nq = seq // bq
    qc, kc, vc = (c // SB_HEAD_DIM for c in (_IN_SBQ, _IN_SBK, _IN_SBV))
    return pl.pallas_call(
        functools.partial(_sb_attn_kernel, bq=bq, bk=bk),
        out_shape=jax.ShapeDtypeStruct((t, _SBW), BF16),
        grid=(batch, SB_HEADS, nq),
        in_specs=[pl.BlockSpec((bq, SB_HEAD_DIM), lambda b, h, i: (b * nq + i, qc + h)),
                  pl.BlockSpec((seq, SB_HEAD_DIM), lambda b, h, i: (b, kc + h)),
                  pl.BlockSpec((seq, SB_HEAD_DIM), lambda b, h, i: (b, vc + h))],
        out_specs=pl.BlockSpec((bq, SB_HEAD_DIM), lambda b, h, i: (b * nq + i, h)),
        scratch_shapes=[pltpu.VMEM((bq, LANES), F32), pltpu.VMEM((bq, SB_HEAD_DIM), F32)],
        compiler_params=_params(3),
        name="sb_attn",
    )(proj, proj, proj)


def _mla_attn_kernel(q_ref, k_ref, v_ref, o_ref, m_ref, acc_ref, *, bq, bk, sub):
    i = pl.program_id(2)
    n_diag = bq // bk
    m_ref[...] = jnp.full(m_ref.shape, -jnp.inf, F32)
    acc_ref[...] = jnp.zeros(acc_ref.shape, F32)
    ones = jnp.ones((bk, LANES), BF16)

    def tile(kb, diag):
        start = pl.multiple_of(kb * bk, bk)
        k = k_ref[pl.ds(start, bk), :]
        v1 = jnp.concatenate([v_ref[pl.ds(start, bk), :], ones], axis=1)
        for s in range(bq // sub):
            rows = slice(s * sub, (s + 1) * sub)
            if diag is not None and (s + 1) * sub <= diag * bk:
                continue
            sc = _dot_nt(q_ref[rows, :], k)
            if diag is not None and s * sub < (diag + 1) * bk:
                qchunk = (lax.broadcasted_iota(jnp.int32, (sub, bk), 0) + s * sub) // CHUNK
                kchunk = (lax.broadcasted_iota(jnp.int32, (sub, bk), 1) + diag * bk) // CHUNK
                sc = jnp.where(kchunk <= qchunk, sc, -jnp.inf)
            m = m_ref[rows, :]
            m_new = jnp.maximum(m, jnp.max(sc, axis=1, keepdims=True))
            alpha = jnp.exp2(m - m_new)
            p = jnp.exp2(sc - jnp.tile(m_new, (1, bk // LANES)))
            acc_ref[rows, :] = (jnp.tile(alpha, (1, 2)) * acc_ref[rows, :]
                                + _dot(p.astype(BF16), v1))
            m_ref[rows, :] = m_new

    for d in range(n_diag):
        tile(i * n_diag + d, d)
    n_before = i * n_diag

    def body(t, carry):
        tile(n_before - 1 - t, None)
        return carry

    lax.fori_loop(0, n_before, body, 0)
    o_ref[...] = (acc_ref[:, 0:MLA_V] / acc_ref[:, MLA_V:2 * MLA_V]).astype(o_ref.dtype)


def _mla_attn(q, k, v, batch, seq, bq, bk, sub):
    t = q.shape[0]
    nq = seq // bq
    return pl.pallas_call(
        functools.partial(_mla_attn_kernel, bq=bq, bk=bk, sub=sub),
        out_shape=jax.ShapeDtypeStruct((t, MLA_HEADS * MLA_V), BF16),
        grid=(batch, MLA_HEADS, nq),
        in_specs=[pl.BlockSpec((bq, MLA_QK_PAD), lambda b, h, i: (b * nq + i, h)),
                  pl.BlockSpec((seq, MLA_QK_PAD), lambda b, h, i: (b, h)),
                  pl.BlockSpec((seq, MLA_V), lambda b, h, i: (b, h))],
        out_specs=pl.BlockSpec((bq, MLA_V), lambda b, h, i: (b * nq + i, h)),
        scratch_shapes=[pltpu.VMEM((bq, LANES), F32), pltpu.VMEM((bq, 2 * MLA_V), F32)],
        compiler_params=_params(3),
        name="mla_attn",
    )(q, k, v)


def _merge_kernel(osb_ref, omla_ref, wsb_ref, wmla_ref, gsb_ref, gmla_ref, bsb_ref, bmla_ref,
                  o_ref):
    gate_sb = jax.nn.sigmoid(gsb_ref[...].astype(F32) + bsb_ref[...])
    gate_mla = jax.nn.sigmoid(gmla_ref[...].astype(F32) + bmla_ref[...])
    mixed = (gate_sb * _dot(osb_ref[...], wsb_ref[...])
             + gate_mla * _dot(omla_ref[...], wmla_ref[...]))
    o_ref[...] = mixed.astype(o_ref.dtype)


def _merge(o_sb, o_mla, w_sb, w_mla, layer, proj, gsb_col, gmla_col, b_gate, bm, bn):
    t, k = o_sb.shape
    d = w_sb.shape[2]
    nj = d // bn
    return pl.pallas_call(
        _merge_kernel,
        out_shape=jax.ShapeDtypeStruct((t, d), BF16),
        grid=(t // bm, nj),
        in_specs=[pl.BlockSpec((bm, k), lambda i, j: (i, 0)),
                  pl.BlockSpec((bm, k), lambda i, j: (i, 0)),
                  pl.BlockSpec((None, k, bn), lambda i, j: (layer, 0, j)),
                  pl.BlockSpec((None, k, bn), lambda i, j: (layer, 0, j)),
                  pl.BlockSpec((bm, bn), lambda i, j: (i, gsb_col // bn + j)),
                  pl.BlockSpec((bm, bn), lambda i, j: (i, gmla_col // bn + j)),
                  pl.BlockSpec((1, bn), lambda i, j: (0, j)),
                  pl.BlockSpec((1, bn), lambda i, j: (0, nj + j))],
        out_specs=pl.BlockSpec((bm, bn), lambda i, j: (i, j)),
        compiler_params=_params(2),
        name="merge",
    )(o_sb, o_mla, w_sb, w_mla, proj, proj, b_gate, b_gate)


def _residual_norm_kernel(a_ref, w_ref, x_ref, g_ref, o_ref, h_ref):
    out = x_ref[...] + _dot(a_ref[...], w_ref[...])
    o_ref[...] = out
    h_ref[...] = _rms(out, g_ref[...]).astype(h_ref.dtype)


def _residual_norm(a, w, layer, x, g, bm):
    t, k = a.shape
    d = w.shape[2]
    return pl.pallas_call(
        _residual_norm_kernel,
        out_shape=(jax.ShapeDtypeStruct((t, d), F32), jax.ShapeDtypeStruct((t, d), BF16)),
        grid=(t // bm,),
        in_specs=[pl.BlockSpec((bm, k), lambda i: (i, 0)),
                  pl.BlockSpec((None, k, d), lambda i: (layer, 0, 0)),
                  pl.BlockSpec((bm, d), lambda i: (i, 0)),
                  pl.BlockSpec((1, d), lambda i: (0, 0))],
        out_specs=(pl.BlockSpec((bm, d), lambda i: (i, 0)),
                   pl.BlockSpec((bm, d), lambda i: (i, 0))),
        compiler_params=_params(1),
        name="residual_norm",
    )(a, w, x, g.reshape(1, d))


def _residual_kernel(a_ref, w_ref, x_ref, o_ref):
    o_ref[...] = x_ref[...] + _dot(a_ref[...], w_ref[...])


def _residual(a, w, layer, x, bm, bn):
    t, k = a.shape
    d = w.shape[2]
    return pl.pallas_call(
        _residual_kernel,
        out_shape=jax.ShapeDtypeStruct((t, d), F32),
        grid=(d // bn, t // bm),
        in_specs=[pl.BlockSpec((bm, k), lambda j, i: (i, 0)),
                  pl.BlockSpec((None, k, bn), lambda j, i: (layer, 0, j)),
                  pl.BlockSpec((bm, bn), lambda j, i: (i, j))],
        out_specs=pl.BlockSpec((bm, bn), lambda j, i: (i, j)),
        compiler_params=_params(2),
        name="ffn_down",
    )(a, w, x)


def _ffn_up_kernel(h_ref, wg_ref, wv_ref, cwg_ref, cwv_ref, cbg_ref, cbv_ref, o_ref,
                   sg_ref, sv_ref, wgb_ref, wvb_ref, *, bm, chunk, blocks_per_seq):
    i = pl.program_id(1)

    @pl.when(i == 0)
    def _():
        wgb_ref[...] = wg_ref[...].astype(wgb_ref.dtype)
        wvb_ref[...] = wv_ref[...].astype(wvb_ref.dtype)

    starts_sequence = (i % blocks_per_seq) == 0
    for s_ref in (sg_ref, sv_ref):
        s_ref[0:SUBLANES, :] = jnp.where(starts_sequence, 0.0, s_ref[bm:bm + SUBLANES, :])

    def conv(s_ref, cw_ref, cb_ref, lo):
        acc = cb_ref[...] + cw_ref[0:1, :] * s_ref[lo - 2:lo - 2 + chunk, :]
        acc = acc + cw_ref[1:2, :] * s_ref[lo - 1:lo - 1 + chunk, :]
        return acc + cw_ref[2:3, :] * s_ref[lo:lo + chunk, :]

    for c in range(bm // chunk):
        h = h_ref[c * chunk:(c + 1) * chunk, :]
        lo = SUBLANES + c * chunk
        sg_ref[lo:lo + chunk, :] = _dot(h, wgb_ref[...])
        sv_ref[lo:lo + chunk, :] = _dot(h, wvb_ref[...])
        gate = conv(sg_ref, cwg_ref, cbg_ref, lo)
        val = conv(sv_ref, cwv_ref, cbv_ref, lo)
        o_ref[c * chunk:(c + 1) * chunk, :] = (
            gate * jax.nn.sigmoid(gate) * val).astype(o_ref.dtype)


def _ffn_up(h, w_up, layer, conv_w, conv_b, seq, bm, bn, chunk):
    t, d = h.shape
    d_ff = w_up.shape[2] // 2
    nj = d_ff // bn
    return pl.pallas_call(
        functools.partial(_ffn_up_kernel, bm=bm, chunk=chunk, blocks_per_seq=seq // bm),
        out_shape=jax.ShapeDtypeStruct((t, d_ff), BF16),
        grid=(nj, t // bm),
        in_specs=[pl.BlockSpec((bm, d), lambda j, i: (i, 0)),
                  pl.BlockSpec((None, d, bn), lambda j, i: (layer, 0, j)),
                  pl.BlockSpec((None, d, bn), lambda j, i: (layer, 0, nj + j)),
                  pl.BlockSpec((CONV_WIDTH, bn), lambda j, i: (0, j)),
                  pl.BlockSpec((CONV_WIDTH, bn), lambda j, i: (0, nj + j)),
                  pl.BlockSpec((1, bn), lambda j, i: (0, j)),
                  pl.BlockSpec((1, bn), lambda j, i: (0, nj + j))],
        out_specs=pl.BlockSpec((bm, bn), lambda j, i: (i, j)),
        scratch_shapes=[pltpu.VMEM((bm + SUBLANES, bn), F32),
                        pltpu.VMEM((bm + SUBLANES, bn), F32),
                        pltpu.VMEM((d, bn), BF16), pltpu.VMEM((d, bn), BF16)],
        compiler_params=_params(2),
        name="ffn_up",
    )(h, w_up, w_up, conv_w, conv_w, conv_b, conv_b)


def _rope_tables(seq):
    half = MLA_ROPE // 2
    inv = 1.0 / (ROPE_THETA ** (jnp.arange(0, MLA_ROPE, 2, dtype=F32) / MLA_ROPE))
    ang = jnp.arange(seq, dtype=F32)[:, None] * inv[None, :]
    cos, sin = jnp.cos(ang), jnp.sin(ang)
    zero = jnp.zeros((seq, half), F32)
    tail = jnp.zeros((seq, LANES - MLA_ROPE), F32)
    c = jnp.concatenate([cos, cos, tail], axis=1)
    s1 = jnp.concatenate([-sin, zero, tail], axis=1)
    s2 = jnp.concatenate([zero, sin, tail], axis=1)
    return c, s1, s2


def _block(n, want):
    b = min(n, want)
    assert n % b == 0, (n, want)
    return b


def _in_layout(d_model):
    src = {"sbq": 0, "sbk": _SBW, "sbv": 2 * _SBW, "cq": 3 * _SBW}
    src["ckv"] = src["cq"] + MLA_Q_LORA
    src["krope"] = src["ckv"] + MLA_KV_LORA
    src["gsb"] = src["krope"] + MLA_ROPE
    src["gmla"] = src["gsb"] + d_model
    src["end"] = src["gmla"] + d_model
    dst = {"sbq": _IN_SBQ, "sbk": _IN_SBK, "sbv": _IN_SBV, "gsb": _IN_GSB,
           "gmla": _IN_GSB + d_model}
    dst["ckv"] = dst["gmla"] + d_model
    dst["cq"] = dst["ckv"] + MLA_KV_LORA
    dst["krope"] = dst["cq"] + MLA_Q_LORA
    dst["end"] = -(-(dst["krope"] + LANES) // _IN_BLOCK) * _IN_BLOCK
    assert dst["ckv"] % MLA_KV_LORA == 0 and dst["cq"] % MLA_Q_LORA == 0
    assert dst["krope"] % LANES == 0 and dst["gmla"] % _IN_BLOCK == 0
    return src, dst


def _prep_w_in_starts(d_model):
    src, dst = _in_layout(d_model)
    widths = {"sbq": _SBW, "sbk": _SBW, "sbv": _SBW, "gsb": d_model, "gmla": d_model,
              "ckv": MLA_KV_LORA, "cq": MLA_Q_LORA}
    starts = []
    for b in range(dst["end"] // _IN_BLOCK):
        r = b * _IN_BLOCK
        name = max((n for n in widths if dst[n] <= r), key=lambda n: dst[n])
        assert r + _IN_BLOCK <= dst[name] + widths[name] or b == dst["end"] // _IN_BLOCK - 1
        starts.append(src[name] + r - dst[name])
    return tuple(starts)


def _prep_w_in_kernel(a_ref, kr_ref, o_ref, *, d_model):
    _, dst = _in_layout(d_model)
    b = pl.program_id(1)
    n_blocks = dst["end"] // _IN_BLOCK
    n_scaled = _SBW // _IN_BLOCK
    cq_rows = dst["krope"] - (n_blocks - 1) * _IN_BLOCK

    @pl.when(b < n_scaled)
    def _():
        o_ref[...] = (a_ref[0] * (SB_HEAD_DIM ** -0.5 * LOG2E)).astype(o_ref.dtype)

    @pl.when((b >= n_scaled) & (b < n_blocks - 1))
    def _():
        o_ref[...] = a_ref[0].astype(o_ref.dtype)

    @pl.when(b == n_blocks - 1)
    def _():
        o_ref[0:cq_rows, :] = a_ref[0, 0:cq_rows, :].astype(o_ref.dtype)
        o_ref[cq_rows:cq_rows + MLA_ROPE, :] = kr_ref[0].astype(o_ref.dtype)
        o_ref[cq_rows + MLA_ROPE:, :] = jnp.zeros(
            (_IN_BLOCK - cq_rows - MLA_ROPE, o_ref.shape[1]), o_ref.dtype)


def _prep_w_in(w_in):
    depth, k, n = w_in.shape
    src, dst = _in_layout(k)
    assert src["end"] == n
    starts = _prep_w_in_starts(k)
    w_t = jnp.swapaxes(w_in, 1, 2)

    def a_map(l, b):
        start = jnp.int32(starts[-1])
        for blk in range(len(starts) - 2, -1, -1):
            start = jnp.where(b == blk, starts[blk], start)
        return (l, pl.multiple_of(start, MLA_ROPE), 0)

    return pl.pallas_call(
        functools.partial(_prep_w_in_kernel, d_model=k),
        out_shape=jax.ShapeDtypeStruct((depth, dst["end"], k), BF16),
        grid=(depth, len(starts)),
        in_specs=[pl.BlockSpec((pl.Element(1), pl.Element(_IN_BLOCK), pl.Element(k)), a_map),
                  pl.BlockSpec((pl.Element(1), pl.Element(MLA_ROPE), pl.Element(k)),
                               lambda l, b: (l, src["krope"], 0))],
        out_specs=pl.BlockSpec((None, _IN_BLOCK, k), lambda l, b: (l, b, 0)),
        compiler_params=_params(2),
        name="prep_w_in",
    )(w_t, w_t)


def _cast_kernel(w_ref, o_ref):
    o_ref[...] = w_ref[...].astype(o_ref.dtype)


def _cast_bf16(w, rows):
    depth, k, n = w.shape
    return pl.pallas_call(
        _cast_kernel,
        out_shape=jax.ShapeDtypeStruct(w.shape, BF16),
        grid=(depth, k // rows),
        in_specs=[pl.BlockSpec((None, rows, n), lambda l, i: (l, i, 0))],
        out_specs=pl.BlockSpec((None, rows, n), lambda l, i: (l, i, 0)),
        compiler_params=_params(2),
        name="cast_bf16",
    )(w)


def _prep_w_uq(w_uq):
    per = MLA_NOPE + MLA_ROPE
    w = w_uq.reshape(w_uq.shape[0], MLA_HEADS, per) * (per ** -0.5 * LOG2E)
    w = jnp.pad(w, ((0, 0), (0, 0), (0, MLA_QK_PAD - per)))
    return w.reshape(w_uq.shape[0], MLA_HEADS * MLA_QK_PAD).astype(BF16)


def _prep_w_ukv(w_ukv):
    w = w_ukv.reshape(w_ukv.shape[0], MLA_HEADS, MLA_NOPE + MLA_V)
    k = w[:, :, :MLA_NOPE].reshape(w_ukv.shape[0], MLA_HEADS * MLA_NOPE)
    v = w[:, :, MLA_NOPE:].reshape(w_ukv.shape[0], MLA_HEADS * MLA_V)
    return jnp.concatenate([k, v], axis=1).astype(BF16)


def kernel(x, norm1_g, w_in, b_gate, q_norm_g, w_uq, kv_norm_g, w_ukv, w_proj_sb, w_proj_mla,
           w_out, norm2_g, w_up, conv_w, conv_b, w_down, final_g):
    batch, seq, d_model = x.shape
    depth = w_in.shape[0]
    t = batch * seq
    tabs = _rope_tables(seq)

    bm_big = _block(seq, 1024)
    bm_mid = _block(seq, 512)
    sb_bq, sb_bk = _block(seq, 2048), _block(seq, 256)
    mla_bq = _block(seq, 2048)
    mla_bk, mla_sub = _block(mla_bq, 1024), _block(mla_bq, 256)

    _, cols = _in_layout(d_model)
    w_in_b = _prep_w_in(w_in)
    w_sb_b, w_mla_b = _cast_bf16(w_proj_sb, 512), _cast_bf16(w_proj_mla, 512)
    w_out_b, w_down_b = _cast_bf16(w_out, 512), _cast_bf16(w_down, 512)

    xf = x.reshape(t, d_model)
    for l in range(depth):
        proj = _norm_matmul(xf, norm1_g[l], w_in_b, l, bm_big, _IN_BLOCK)
        q_mla = _mla_q(proj, cols["cq"], q_norm_g[l], _prep_w_uq(w_uq[l]), tabs, bm_mid, seq)
        k_mla, v_mla = _mla_kv(proj, cols["ckv"], cols["krope"], kv_norm_g[l],
                               _prep_w_ukv(w_ukv[l]), tabs, bm_mid, seq)
        o_sb = _sb_attn(proj, batch, seq, sb_bq, sb_bk)
        o_mla = _mla_attn(q_mla, k_mla, v_mla, batch, seq, mla_bq, mla_bk, mla_sub)
        mixed = _merge(o_sb, o_mla, w_sb_b, w_mla_b, l, proj, cols["gsb"], cols["gmla"],
                       b_gate[l].reshape(1, -1), bm_big, 2 * _IN_BLOCK)
        x1, h2 = _residual_norm(mixed, w_out_b, l, xf, norm2_g[l], bm_mid)
        act = _ffn_up(h2, w_up, l, conv_w[l], conv_b[l].reshape(1, -1), seq,
                      bm_big, 512, _block(bm_big, 128))
        xf = _residual(act, w_down_b, l, x1, bm_mid, 1024)
    return _rmsnorm(xf, final_g, bm_mid).reshape(batch, seq, d_model)
```

```python
import functools
import math

import jax
import jax.numpy as jnp
from jax import lax
from jax.experimental import pallas as pl
from jax.experimental.pallas import tpu as pltpu

F32 = jnp.float32
BF16 = jnp.bfloat16

EPS = 1e-6
LOG2E = math.log2(math.e)
ROPE_THETA = 10000.0
CHUNK = 64
SB_HEADS = 8
SB_HEAD_DIM = 128
MLA_HEADS = 8
MLA_NOPE = 128
MLA_ROPE = 64
MLA_V = 128
MLA_Q_LORA = 768
MLA_KV_LORA = 512
CONV_WIDTH = 3

LANES = 128
SUBLANES = 8
MLA_QK_PAD = 2 * LANES
VMEM_LIMIT_BYTES = 56 * 1024 * 1024

_SBW = SB_HEADS * SB_HEAD_DIM
_IN_SBQ = 0
_IN_SBK = _SBW
_IN_SBV = 2 * _SBW
_IN_GSB = 3 * _SBW
_IN_BLOCK = 512


def _params(n_axes):
    return pltpu.CompilerParams(
        dimension_semantics=("arbitrary",) * n_axes, vmem_limit_bytes=VMEM_LIMIT_BYTES)


def _rms(x, g):
    return x * lax.rsqrt(jnp.mean(x * x, axis=-1, keepdims=True) + EPS) * g


def _rope_group(g, c, s1, s2):
    return g * c + pltpu.roll(g, 96, 1) * s1 + pltpu.roll(g, 32, 1) * s2


def _dot(a, b):
    return jnp.dot(a, b, preferred_element_type=F32)


def _dot_nt(a, b):
    return lax.dot_general(a, b, (((1,), (1,)), ((), ())), preferred_element_type=F32)


def _rmsnorm_kernel(x_ref, g_ref, o_ref):
    o_ref[...] = _rms(x_ref[...], g_ref[...]).astype(o_ref.dtype)


def _rmsnorm(x, g, bm):
    t, d = x.shape
    return pl.pallas_call(
        _rmsnorm_kernel,
        out_shape=jax.ShapeDtypeStruct((t, d), x.dtype),
        grid=(t // bm,),
        in_specs=[pl.BlockSpec((bm, d), lambda i: (i, 0)),
                  pl.BlockSpec((1, d), lambda i: (0, 0))],
        out_specs=pl.BlockSpec((bm, d), lambda i: (i, 0)),
        compiler_params=_params(1),
        name="rmsnorm",
    )(x, g.reshape(1, d))


def _norm_matmul_kernel(x_ref, g_ref, w_ref, o_ref, h_ref):
    @pl.when(pl.program_id(1) == 0)
    def _():
        h_ref[...] = _rms(x_ref[...], g_ref[...]).astype(h_ref.dtype)

    o_ref[...] = _dot_nt(h_ref[...], w_ref[...]).astype(o_ref.dtype)


def _norm_matmul(x, g, w_t, layer, bm, bn):
    t, k = x.shape
    n = w_t.shape[1]
    return pl.pallas_call(
        _norm_matmul_kernel,
        out_shape=jax.ShapeDtypeStruct((t, n), BF16),
        grid=(t // bm, n // bn),
        in_specs=[pl.BlockSpec((bm, k), lambda i, j: (i, 0)),
                  pl.BlockSpec((1, k), lambda i, j: (0, 0)),
                  pl.BlockSpec((None, bn, k), lambda i, j: (layer, j, 0))],
        out_specs=pl.BlockSpec((bm, bn), lambda i, j: (i, j)),
        scratch_shapes=[pltpu.VMEM((bm, k), BF16)],
        compiler_params=_params(2),
        name="in_proj",
    )(x, g.reshape(1, k), w_t)


def _mla_q_kernel(cq_ref, g_ref, w_ref, c_ref, s1_ref, s2_ref, o_ref):
    h = _rms(cq_ref[...].astype(F32), g_ref[...]).astype(BF16)
    y = _dot(h, w_ref[...])
    c, s1, s2 = c_ref[...], s1_ref[...], s2_ref[...]
    for hd in range(MLA_HEADS):
        lo = hd * MLA_QK_PAD
        o_ref[:, lo:lo + LANES] = y[:, lo:lo + LANES].astype(o_ref.dtype)
        rot = _rope_group(y[:, lo + LANES:lo + MLA_QK_PAD], c, s1, s2)
        o_ref[:, lo + LANES:lo + MLA_QK_PAD] = rot.astype(o_ref.dtype)


def _mla_q(proj, cq_col, g, w, tabs, bm, seq):
    t = proj.shape[0]
    n = w.shape[1]
    nsb = seq // bm
    tab_spec = pl.BlockSpec((bm, LANES), lambda i: (i % nsb, 0))
    return pl.pallas_call(
        _mla_q_kernel,
        out_shape=jax.ShapeDtypeStruct((t, n), BF16),
        grid=(t // bm,),
        in_specs=[pl.BlockSpec((bm, MLA_Q_LORA), lambda i: (i, cq_col // MLA_Q_LORA)),
                  pl.BlockSpec((1, MLA_Q_LORA), lambda i: (0, 0)),
                  pl.BlockSpec((MLA_Q_LORA, n), lambda i: (0, 0)),
                  tab_spec, tab_spec, tab_spec],
        out_specs=pl.BlockSpec((bm, n), lambda i: (i, 0)),
        compiler_params=_params(1),
        name="mla_q",
    )(proj, g.reshape(1, MLA_Q_LORA), w, *tabs)


def _mla_kv_kernel(ckv_ref, kr_ref, g_ref, w_ref, c_ref, s1_ref, s2_ref, k_ref, v_ref):
    h = _rms(ckv_ref[...].astype(F32), g_ref[...]).astype(BF16)
    y = _dot(h, w_ref[...])
    rot = _rope_group(kr_ref[...].astype(F32), c_ref[...], s1_ref[...], s2_ref[...])
    rot = rot.astype(k_ref.dtype)
    kw = MLA_HEADS * MLA_NOPE
    for hd in range(MLA_HEADS):
        lo = hd * MLA_QK_PAD
        k_ref[:, lo:lo + LANES] = y[:, hd * MLA_NOPE:(hd + 1) * MLA_NOPE].astype(k_ref.dtype)
        k_ref[:, lo + LANES:lo + MLA_QK_PAD] = rot
    v_ref[...] = y[:, kw:].astype(v_ref.dtype)


def _mla_kv(proj, ckv_col, kr_col, g, w, tabs, bm, seq):
    t = proj.shape[0]
    nsb = seq // bm
    tab_spec = pl.BlockSpec((bm, LANES), lambda i: (i % nsb, 0))
    return pl.pallas_call(
        _mla_kv_kernel,
        out_shape=(jax.ShapeDtypeStruct((t, MLA_HEADS * MLA_QK_PAD), BF16),
                   jax.ShapeDtypeStruct((t, MLA_HEADS * MLA_V), BF16)),
        grid=(t // bm,),
        in_specs=[pl.BlockSpec((bm, MLA_KV_LORA), lambda i: (i, ckv_col // MLA_KV_LORA)),
                  pl.BlockSpec((bm, LANES), lambda i: (i, kr_col // LANES)),
                  pl.BlockSpec((1, MLA_KV_LORA), lambda i: (0, 0)),
                  pl.BlockSpec(w.shape, lambda i: (0, 0)),
                  tab_spec, tab_spec, tab_spec],
        out_specs=(pl.BlockSpec((bm, MLA_HEADS * MLA_QK_PAD), lambda i: (i, 0)),
                   pl.BlockSpec((bm, MLA_HEADS * MLA_V), lambda i: (i, 0))),
        compiler_params=_params(1),
        name="mla_kv",
    )(proj, proj, g.reshape(1, MLA_KV_LORA), w, *tabs)


def _sb_attn_kernel(q_ref, k_ref, v_ref, o_ref, run_ref, acc_ref, *, bq, bk):
    i = pl.program_id(2)
    n_diag = bq // bk
    row = lax.broadcasted_iota(jnp.int32, (bk, bk), 0)
    col = lax.broadcasted_iota(jnp.int32, (bk, bk), 1)
    later = (row > col).astype(BF16)
    later2 = jnp.concatenate([later, later], axis=0)

    run_ref[...] = jnp.zeros(run_ref.shape, F32)
    acc_ref[...] = jnp.zeros(acc_ref.shape, F32)

    def tile(kb, diag):
        rows = slice(0 if diag is None else diag * bk, bq)
        start = pl.multiple_of(kb * bk, bk)
        k = k_ref[pl.ds(start, bk), :]
        v = v_ref[pl.ds(start, bk), :]
        z = _dot_nt(q_ref[rows, :], k)
        z_pos = jnp.maximum(z, 0.0)
        z_neg = z - z_pos
        log_term = jnp.log(1.0 + jnp.exp2(z_neg - z_pos)) * LOG2E
        sp = z_pos + log_term
        if diag is not None:
            causal = (lax.broadcasted_iota(jnp.int32, z.shape, 1)
                      < lax.broadcasted_iota(jnp.int32, z.shape, 0))
            sp = jnp.where(causal, sp, 0.0)
        hi = sp.astype(BF16)
        split = jnp.concatenate([hi, (sp - hi.astype(F32)).astype(BF16)], axis=1)
        suffix = _dot(split, later2)
        a = jnp.exp2(z_neg - log_term - suffix)
        if diag is not None:
            a = jnp.where(causal, a, 0.0)
        run = run_ref[rows, :]
        acc_ref[rows, :] += jnp.exp2(-run) * _dot(a.astype(BF16), v)
        run_ref[rows, :] = run + (suffix[:, 0:1] + sp[:, 0:1])

    for d in range(n_diag - 1, -1, -1):
        tile(i * n_diag + d, d)
    n_before = i * n_diag

    def body(t, carry):
        tile(n_before - 1 - t, None)
        return carry

    lax.fori_loop(0, n_before, body, 0)
    o_ref[...] = acc_ref[...].astype(o_ref.dtype)


def _sb_attn(proj, batch, seq, bq, bk):
    t = proj.shape[0]
    nq = seq // bq
    qc, kc, vc = (c // SB_HEAD_DIM for c in (_IN_SBQ, _IN_SBK, _IN_SBV))
    return pl.pallas_call(
        functools.partial(_sb_attn_kernel, bq=bq, bk=bk),
        out_shape=jax.ShapeDtypeStruct((t, _SBW), BF16),
        grid=(batch, SB_HEADS, nq),
        in_specs=[pl.BlockSpec((bq, SB_HEAD_DIM), lambda b, h, i: (b * nq + i, qc + h)),
                  pl.BlockSpec((seq, SB_HEAD_DIM), lambda b, h, i: (b, kc + h)),
                  pl.BlockSpec((seq, SB_HEAD_DIM), lambda b, h, i: (b, vc + h))],
        out_specs=pl.BlockSpec((bq, SB_HEAD_DIM), lambda b, h, i: (b * nq + i, h)),
        scratch_shapes=[pltpu.VMEM((bq, LANES), F32), pltpu.VMEM((bq, SB_HEAD_DIM), F32)],
        compiler_params=_params(3),
        name="sb_attn",
    )(proj, proj, proj)


def _mla_attn_kernel(q_ref, k_ref, v_ref, o_ref, m_ref, acc_ref, *, bq, bk, sub):
    i = pl.program_id(2)
    n_diag = bq // bk
    m_ref[...] = jnp.full(m_ref.shape, -jnp.inf, F32)
    acc_ref[...] = jnp.zeros(acc_ref.shape, F32)
    ones = jnp.ones((bk, LANES), BF16)

    def tile(kb, diag):
        start = pl.multiple_of(kb * bk, bk)
        k = k_ref[pl.ds(start, bk), :]
        v1 = jnp.concatenate([v_ref[pl.ds(start, bk), :], ones], axis=1)
        for s in range(bq // sub):
            rows = slice(s * sub, (s + 1) * sub)
            if diag is not None and (s + 1) * sub <= diag * bk:
                continue
            sc = _dot_nt(q_ref[rows, :], k)
            if diag is not None and s * sub < (diag + 1) * bk:
                qchunk = (lax.broadcasted_iota(jnp.int32, (sub, bk), 0) + s * sub) // CHUNK
                kchunk = (lax.broadcasted_iota(jnp.int32, (sub, bk), 1) + diag * bk) // CHUNK
                sc = jnp.where(kchunk <= qchunk, sc, -jnp.inf)
            m = m_ref[rows, :]
            m_new = jnp.maximum(m, jnp.max(sc, axis=1, keepdims=True))
            alpha = jnp.exp2(m - m_new)
            p = jnp.exp2(sc - jnp.tile(m_new, (1, bk // LANES)))
            acc_ref[rows, :] = (jnp.tile(alpha, (1, 2)) * acc_ref[rows, :]
                                + _dot(p.astype(BF16), v1))
            m_ref[rows, :] = m_new

    for d in range(n_diag):
        tile(i * n_diag + d, d)
    n_before = i * n_diag

    def body(t, carry):
        tile(n_before - 1 - t, None)
        return carry

    lax.fori_loop(0, n_before, body, 0)
    o_ref[...] = (acc_ref[:, 0:MLA_V] / acc_ref[:, MLA_V:2 * MLA_V]).astype(o_ref.dtype)


def _mla_attn(q, k, v, batch, seq, bq, bk, sub):
    t = q.shape[0]
    nq = seq // bq
    return pl.pallas_call(
        functools.partial(_mla_attn_kernel, bq=bq, bk=bk, sub=sub),
        out_shape=jax.ShapeDtypeStruct((t, MLA_HEADS * MLA_V), BF16),
        grid=(batch, MLA_HEADS, nq),
        in_specs=[pl.BlockSpec((bq, MLA_QK_PAD), lambda b, h, i: (b * nq + i, h)),
                  pl.BlockSpec((seq, MLA_QK_PAD), lambda b, h, i: (b, h)),
                  pl.BlockSpec((seq, MLA_V), lambda b, h, i: (b, h))],
        out_specs=pl.BlockSpec((bq, MLA_V), lambda b, h, i: (b * nq + i, h)),
        scratch_shapes=[pltpu.VMEM((bq, LANES), F32), pltpu.VMEM((bq, 2 * MLA_V), F32)],
        compiler_params=_params(3),
        name="mla_attn",
    )(q, k, v)


def _merge_kernel(osb_ref, omla_ref, wsb_ref, wmla_ref, gsb_ref, gmla_ref, bsb_ref, bmla_ref,
                  o_ref):
    gate_sb = jax.nn.sigmoid(gsb_ref[...].astype(F32) + bsb_ref[...])
    gate_mla = jax.nn.sigmoid(gmla_ref[...].astype(F32) + bmla_ref[...])
    mixed = (gate_sb * _dot(osb_ref[...], wsb_ref[...])
             + gate_mla * _dot(omla_ref[...], wmla_ref[...]))
    o_ref[...] = mixed.astype(o_ref.dtype)


def _merge(o_sb, o_mla, w_sb, w_mla, layer, proj, gsb_col, gmla_col, b_gate, bm, bn):
    t, k = o_sb.shape
    d = w_sb.shape[2]
    nj = d // bn
    return pl.pallas_call(
        _merge_kernel,
        out_shape=jax.ShapeDtypeStruct((t, d), BF16),
        grid=(t // bm, nj),
        in_specs=[pl.BlockSpec((bm, k), lambda i, j: (i, 0)),
                  pl.BlockSpec((bm, k), lambda i, j: (i, 0)),
                  pl.BlockSpec((None, k, bn), lambda i, j: (layer, 0, j)),
                  pl.BlockSpec((None, k, bn), lambda i, j: (layer, 0, j)),
                  pl.BlockSpec((bm, bn), lambda i, j: (i, gsb_col // bn + j)),
                  pl.BlockSpec((bm, bn), lambda i, j: (i, gmla_col // bn + j)),
                  pl.BlockSpec((1, bn), lambda i, j: (0, j)),
                  pl.BlockSpec((1, bn), lambda i, j: (0, nj + j))],
        out_specs=pl.BlockSpec((bm, bn), lambda i, j: (i, j)),
        compiler_params=_params(2),
        name="merge",
    )(o_sb, o_mla, w_sb, w_mla, proj, proj, b_gate, b_gate)


def _residual_norm_kernel(a_ref, w_ref, x_ref, g_ref, o_ref, h_ref):
    out = x_ref[...] + _dot(a_ref[...], w_ref[...])
    o_ref[...] = out
    h_ref[...] = _rms(out, g_ref[...]).astype(h_ref.dtype)


def _residual_norm(a, w, layer, x, g, bm):
    t, k = a.shape
    d = w.shape[2]
    return pl.pallas_call(
        _residual_norm_kernel,
        out_shape=(jax.ShapeDtypeStruct((t, d), F32), jax.ShapeDtypeStruct((t, d), BF16)),
        grid=(t // bm,),
        in_specs=[pl.BlockSpec((bm, k), lambda i: (i, 0)),
                  pl.BlockSpec((None, k, d), lambda i: (layer, 0, 0)),
                  pl.BlockSpec((bm, d), lambda i: (i, 0)),
                  pl.BlockSpec((1, d), lambda i: (0, 0))],
        out_specs=(pl.BlockSpec((bm, d), lambda i: (i, 0)),
                   pl.BlockSpec((bm, d), lambda i: (i, 0))),
        compiler_params=_params(1),
        name="residual_norm",
    )(a, w, x, g.reshape(1, d))


def _residual_kernel(a_ref, w_ref, x_ref, o_ref):
    o_ref[...] = x_ref[...] + _dot(a_ref[...], w_ref[...])


def _residual(a, w, layer, x, bm, bn):
    t, k = a.shape
    d = w.shape[2]
    return pl.pallas_call(
        _residual_kernel,
        out_shape=jax.ShapeDtypeStruct((t, d), F32),
        grid=(d // bn, t // bm),
        in_specs=[pl.BlockSpec((bm, k), lambda j, i: (i, 0)),
                  pl.BlockSpec((None, k, bn), lambda j, i: (layer, 0, j)),
                  pl.BlockSpec((bm, bn), lambda j, i: (i, j))],
        out_specs=pl.BlockSpec((bm, bn), lambda j, i: (i, j)),
        compiler_params=_params(2),
        name="ffn_down",
    )(a, w, x)


def _ffn_up_kernel(h_ref, wg_ref, wv_ref, cwg_ref, cwv_ref, cbg_ref, cbv_ref, o_ref,
                   sg_ref, sv_ref, wgb_ref, wvb_ref, *, bm, chunk, blocks_per_seq):
    i = pl.program_id(1)

    @pl.when(i == 0)
    def _():
        wgb_ref[...] = wg_ref[...].astype(wgb_ref.dtype)
        wvb_ref[...] = wv_ref[...].astype(wvb_ref.dtype)

    starts_sequence = (i % blocks_per_seq) == 0
    for s_ref in (sg_ref, sv_ref):
        s_ref[0:SUBLANES, :] = jnp.where(starts_sequence, 0.0, s_ref[bm:bm + SUBLANES, :])

    def conv(s_ref, cw_ref, cb_ref, cols):
        lo = SUBLANES
        acc = cb_ref[:, cols] + cw_ref[0:1, cols] * s_ref[lo - 2:lo - 2 + bm, cols]
        acc = acc + cw_ref[1:2, cols] * s_ref[lo - 1:lo - 1 + bm, cols]
        return acc + cw_ref[2:3, cols] * s_ref[lo:lo + bm, cols]

    h = h_ref[...]
    for c in range(o_ref.shape[1] // chunk):
        cols = slice(c * chunk, (c + 1) * chunk)
        sg_ref[SUBLANES:SUBLANES + bm, cols] = _dot(h, wgb_ref[:, cols])
        sv_ref[SUBLANES:SUBLANES + bm, cols] = _dot(h, wvb_ref[:, cols])
        gate = conv(sg_ref, cwg_ref, cbg_ref, cols)
        val = conv(sv_ref, cwv_ref, cbv_ref, cols)
        o_ref[:, cols] = (gate * jax.nn.sigmoid(gate) * val).astype(o_ref.dtype)


def _ffn_up(h, w_up, layer, conv_w, conv_b, seq, bm, bn, chunk):
    t, d = h.shape
    d_ff = w_up.shape[2] // 2
    nj = d_ff // bn
    return pl.pallas_call(
        functools.partial(_ffn_up_kernel, bm=bm, chunk=chunk, blocks_per_seq=seq // bm),
        out_shape=jax.ShapeDtypeStruct((t, d_ff), BF16),
        grid=(nj, t // bm),
        in_specs=[pl.BlockSpec((bm, d), lambda j, i: (i, 0)),
                  pl.BlockSpec((None, d, bn), lambda j, i: (layer, 0, j)),
                  pl.BlockSpec((None, d, bn), lambda j, i: (layer, 0, nj + j)),
                  pl.BlockSpec((CONV_WIDTH, bn), lambda j, i: (0, j)),
                  pl.BlockSpec((CONV_WIDTH, bn), lambda j, i: (0, nj + j)),
                  pl.BlockSpec((1, bn), lambda j, i: (0, j)),
                  pl.BlockSpec((1, bn), lambda j, i: (0, nj + j))],
        out_specs=pl.BlockSpec((bm, bn), lambda j, i: (i, j)),
        scratch_shapes=[pltpu.VMEM((bm + SUBLANES, bn), F32),
                        pltpu.VMEM((bm + SUBLANES, bn), F32),
                        pltpu.VMEM((d, bn), BF16), pltpu.VMEM((d, bn), BF16)],
        compiler_params=_params(2),
        name="ffn_up",
    )(h, w_up, w_up, conv_w, conv_w, conv_b, conv_b)


def _rope_tables(seq):
    half = MLA_ROPE // 2
    inv = 1.0 / (ROPE_THETA ** (jnp.arange(0, MLA_ROPE, 2, dtype=F32) / MLA_ROPE))
    ang = jnp.arange(seq, dtype=F32)[:, None] * inv[None, :]
    cos, sin = jnp.cos(ang), jnp.sin(ang)
    zero = jnp.zeros((seq, half), F32)
    tail = jnp.zeros((seq, LANES - MLA_ROPE), F32)
    c = jnp.concatenate([cos, cos, tail], axis=1)
    s1 = jnp.concatenate([-sin, zero, tail], axis=1)
    s2 = jnp.concatenate([zero, sin, tail], axis=1)
    return c, s1, s2


def _block(n, want):
    b = min(n, want)
    assert n % b == 0, (n, want)
    return b


def _in_layout(d_model):
    src = {"sbq": 0, "sbk": _SBW, "sbv": 2 * _SBW, "cq": 3 * _SBW}
    src["ckv"] = src["cq"] + MLA_Q_LORA
    src["krope"] = src["ckv"] + MLA_KV_LORA
    src["gsb"] = src["krope"] + MLA_ROPE
    src["gmla"] = src["gsb"] + d_model
    src["end"] = src["gmla"] + d_model
    dst = {"sbq": _IN_SBQ, "sbk": _IN_SBK, "sbv": _IN_SBV, "gsb": _IN_GSB,
           "gmla": _IN_GSB + d_model}
    dst["ckv"] = dst["gmla"] + d_model
    dst["cq"] = dst["ckv"] + MLA_KV_LORA
    dst["krope"] = dst["cq"] + MLA_Q_LORA
    dst["end"] = -(-(dst["krope"] + LANES) // _IN_BLOCK) * _IN_BLOCK
    assert dst["ckv"] % MLA_KV_LORA == 0 and dst["cq"] % MLA_Q_LORA == 0
    assert dst["krope"] % LANES == 0 and dst["gmla"] % _IN_BLOCK == 0
    return src, dst


def _prep_w_in_starts(d_model):
    src, dst = _in_layout(d_model)
    widths = {"sbq": _SBW, "sbk": _SBW, "sbv": _SBW, "gsb": d_model, "gmla": d_model,
              "ckv": MLA_KV_LORA, "cq": MLA_Q_LORA}
    starts = []
    for b in range(dst["end"] // _IN_BLOCK):
        r = b * _IN_BLOCK
        name = max((n for n in widths if dst[n] <= r), key=lambda n: dst[n])
        assert r + _IN_BLOCK <= dst[name] + widths[name] or b == dst["end"] // _IN_BLOCK - 1
        starts.append(src[name] + r - dst[name])
    return tuple(starts)


def _prep_w_in_kernel(a_ref, kr_ref, o_ref, *, d_model):
    _, dst = _in_layout(d_model)
    b = pl.program_id(1)
    n_blocks = dst["end"] // _IN_BLOCK
    n_scaled = _SBW // _IN_BLOCK
    cq_rows = dst["krope"] - (n_blocks - 1) * _IN_BLOCK

    @pl.when(b < n_scaled)
    def _():
        o_ref[...] = (a_ref[0] * (SB_HEAD_DIM ** -0.5 * LOG2E)).astype(o_ref.dtype)

    @pl.when((b >= n_scaled) & (b < n_blocks - 1))
    def _():
        o_ref[...] = a_ref[0].astype(o_ref.dtype)

    @pl.when(b == n_blocks - 1)
    def _():
        o_ref[0:cq_rows, :] = a_ref[0, 0:cq_rows, :].astype(o_ref.dtype)
        o_ref[cq_rows:cq_rows + MLA_ROPE, :] = kr_ref[0].astype(o_ref.dtype)
        o_ref[cq_rows + MLA_ROPE:, :] = jnp.zeros(
            (_IN_BLOCK - cq_rows - MLA_ROPE, o_ref.shape[1]), o_ref.dtype)


def _prep_w_in(w_in):
    depth, k, n = w_in.shape
    src, dst = _in_layout(k)
    assert src["end"] == n
    starts = _prep_w_in_starts(k)
    w_t = jnp.swapaxes(w_in, 1, 2)

    def a_map(l, b):
        start = jnp.int32(starts[-1])
        for blk in range(len(starts) - 2, -1, -1):
            start = jnp.where(b == blk, starts[blk], start)
        return (l, pl.multiple_of(start, MLA_ROPE), 0)

    return pl.pallas_call(
        functools.partial(_prep_w_in_kernel, d_model=k),
        out_shape=jax.ShapeDtypeStruct((depth, dst["end"], k), BF16),
        grid=(depth, len(starts)),
        in_specs=[pl.BlockSpec((pl.Element(1), pl.Element(_IN_BLOCK), pl.Element(k)), a_map),
                  pl.BlockSpec((pl.Element(1), pl.Element(MLA_ROPE), pl.Element(k)),
                               lambda l, b: (l, src["krope"], 0))],
        out_specs=pl.BlockSpec((None, _IN_BLOCK, k), lambda l, b: (l, b, 0)),
        compiler_params=_params(2),
        name="prep_w_in",
    )(w_t, w_t)


def _cast_kernel(w_ref, o_ref):
    o_ref[...] = w_ref[...].astype(o_ref.dtype)


def _cast_bf16(w, rows):
    depth, k, n = w.shape
    return pl.pallas_call(
        _cast_kernel,
        out_shape=jax.ShapeDtypeStruct(w.shape, BF16),
        grid=(depth, k // rows),
        in_specs=[pl.BlockSpec((None, rows, n), lambda l, i: (l, i, 0))],
        out_specs=pl.BlockSpec((None, rows, n), lambda l, i: (l, i, 0)),
        compiler_params=_params(2),
        name="cast_bf16",
    )(w)


def _prep_w_uq(w_uq):
    per = MLA_NOPE + MLA_ROPE
    w = w_uq.reshape(w_uq.shape[0], MLA_HEADS, per) * (per ** -0.5 * LOG2E)
    w = jnp.pad(w, ((0, 0), (0, 0), (0, MLA_QK_PAD - per)))
    return w.reshape(w_uq.shape[0], MLA_HEADS * MLA_QK_PAD).astype(BF16)


def _prep_w_ukv(w_ukv):
    w = w_ukv.reshape(w_ukv.shape[0], MLA_HEADS, MLA_NOPE + MLA_V)
    k = w[:, :, :MLA_NOPE].reshape(w_ukv.shape[0], MLA_HEADS * MLA_NOPE)
    v = w[:, :, MLA_NOPE:].reshape(w_ukv.shape[0], MLA_HEADS * MLA_V)
    return jnp.concatenate([k, v], axis=1).astype(BF16)


def kernel(x, norm1_g, w_in, b_gate, q_norm_g, w_uq, kv_norm_g, w_ukv, w_proj_sb, w_proj_mla,
           w_out, norm2_g, w_up, conv_w, conv_b, w_down, final_g):
    batch, seq, d_model = x.shape
    depth = w_in.shape[0]
    t = batch * seq
    tabs = _rope_tables(seq)

    bm_big = _block(seq, 1024)
    bm_mid = _block(seq, 512)
    sb_bq, sb_bk = _block(seq, 2048), _block(seq, 256)
    mla_bq = _block(seq, 2048)
    mla_bk, mla_sub = _block(mla_bq, 1024), _block(mla_bq, 256)

    _, cols = _in_layout(d_model)
    w_in_b = _prep_w_in(w_in)
    w_sb_b, w_mla_b = _cast_bf16(w_proj_sb, 512), _cast_bf16(w_proj_mla, 512)
    w_out_b, w_down_b = _cast_bf16(w_out, 512), _cast_bf16(w_down, 512)

    xf = x.reshape(t, d_model)
    for l in range(depth):
        proj = _norm_matmul(xf, norm1_g[l], w_in_b, l, bm_big, _IN_BLOCK)
        q_mla = _mla_q(proj, cols["cq"], q_norm_g[l], _prep_w_uq(w_uq[l]), tabs, bm_mid, seq)
        k_mla, v_mla = _mla_kv(proj, cols["ckv"], cols["krope"], kv_norm_g[l],
                               _prep_w_ukv(w_ukv[l]), tabs, bm_mid, seq)
        o_sb = _sb_attn(proj, batch, seq, sb_bq, sb_bk)
        o_mla = _mla_attn(q_mla, k_mla, v_mla, batch, seq, mla_bq, mla_bk, mla_sub)
        mixed = _merge(o_sb, o_mla, w_sb_b, w_mla_b, l, proj, cols["gsb"], cols["gmla"],
                       b_gate[l].reshape(1, -1), bm_big, 2 * _IN_BLOCK)
        x1, h2 = _residual_norm(mixed, w_out_b, l, xf, norm2_g[l], bm_mid)
        act = _ffn_up(h2, w_up, l, conv_w[l], conv_b[l].reshape(1, -1), seq,
                      bm_big, 512, 256)
        xf = _residual(act, w_down_b, l, x1, bm_mid, 1024)
    return _rmsnorm(xf, final_g, bm_mid).reshape(batch, seq, d_model)
```

```python
import functools
import math

import jax
import jax.numpy as jnp
from jax import lax
from jax.experimental import pallas as pl
from jax.experimental.pallas import tpu as pltpu

F32 = jnp.float32
BF16 = jnp.bfloat16

EPS = 1e-6
LOG2E = math.log2(math.e)
ROPE_THETA = 10000.0
CHUNK = 64
SB_HEADS = 8
SB_HEAD_DIM = 128
MLA_HEADS = 8
MLA_NOPE = 128
MLA_ROPE = 64
MLA_V = 128
MLA_Q_LORA = 768
MLA_KV_LORA = 512
CONV_WIDTH = 3

LANES = 128
SUBLANES = 8
MLA_QK_PAD = 2 * LANES
VMEM_LIMIT_BYTES = 56 * 1024 * 1024

_SBW = SB_HEADS * SB_HEAD_DIM
_IN_SBQ = 0
_IN_SBK = _SBW
_IN_SBV = 2 * _SBW
_IN_GSB = 3 * _SBW
_IN_BLOCK = 512


def _params(n_axes):
    return pltpu.CompilerParams(
        dimension_semantics=("arbitrary",) * n_axes, vmem_limit_bytes=VMEM_LIMIT_BYTES)


def _rms(x, g):
    return x * lax.rsqrt(jnp.mean(x * x, axis=-1, keepdims=True) + EPS) * g


def _rope_group(g, c, s1, s2):
    return g * c + pltpu.roll(g, 96, 1) * s1 + pltpu.roll(g, 32, 1) * s2


def _dot(a, b):
    return jnp.dot(a, b, preferred_element_type=F32)


def _dot_nt(a, b):
    return lax.dot_general(a, b, (((1,), (1,)), ((), ())), preferred_element_type=F32)


def _rmsnorm_kernel(x_ref, g_ref, o_ref):
    o_ref[...] = _rms(x_ref[...], g_ref[...]).astype(o_ref.dtype)


def _rmsnorm(x, g, bm):
    t, d = x.shape
    return pl.pallas_call(
        _rmsnorm_kernel,
        out_shape=jax.ShapeDtypeStruct((t, d), x.dtype),
        grid=(t // bm,),
        in_specs=[pl.BlockSpec((bm, d), lambda i: (i, 0)),
                  pl.BlockSpec((1, d), lambda i: (0, 0))],
        out_specs=pl.BlockSpec((bm, d), lambda i: (i, 0)),
        compiler_params=_params(1),
        name="rmsnorm",
    )(x, g.reshape(1, d))


def _norm_matmul_kernel(x_ref, g_ref, w_ref, o_ref, h_ref):
    @pl.when(pl.program_id(1) == 0)
    def _():
        h_ref[...] = _rms(x_ref[...], g_ref[...]).astype(h_ref.dtype)

    o_ref[...] = _dot_nt(h_ref[...], w_ref[...]).astype(o_ref.dtype)


def _norm_matmul(x, g, w_t, layer, bm, bn):
    t, k = x.shape
    n = w_t.shape[1]
    return pl.pallas_call(
        _norm_matmul_kernel,
        out_shape=jax.ShapeDtypeStruct((t, n), BF16),
        grid=(t // bm, n // bn),
        in_specs=[pl.BlockSpec((bm, k), lambda i, j: (i, 0)),
                  pl.BlockSpec((1, k), lambda i, j: (0, 0)),
                  pl.BlockSpec((None, bn, k), lambda i, j: (layer, j, 0))],
        out_specs=pl.BlockSpec((bm, bn), lambda i, j: (i, j)),
        scratch_shapes=[pltpu.VMEM((bm, k), BF16)],
        compiler_params=_params(2),
        name="in_proj",
    )(x, g.reshape(1, k), w_t)


def _mla_q_kernel(cq_ref, g_ref, w_ref, c_ref, s1_ref, s2_ref, o_ref):
    h = _rms(cq_ref[...].astype(F32), g_ref[...]).astype(BF16)
    y = _dot(h, w_ref[...])
    c, s1, s2 = c_ref[...], s1_ref[...], s2_ref[...]
    for hd in range(MLA_HEADS):
        lo = hd * MLA_QK_PAD
        o_ref[:, lo:lo + LANES] = y[:, lo:lo + LANES].astype(o_ref.dtype)
        rot = _rope_group(y[:, lo + LANES:lo + MLA_QK_PAD], c, s1, s2)
        o_ref[:, lo + LANES:lo + MLA_QK_PAD] = rot.astype(o_ref.dtype)


def _mla_q(proj, cq_col, g, w, tabs, bm, seq):
    t = proj.shape[0]
    n = w.shape[1]
    nsb = seq // bm
    tab_spec = pl.BlockSpec((bm, LANES), lambda i: (i % nsb, 0))
    return pl.pallas_call(
        _mla_q_kernel,
        out_shape=jax.ShapeDtypeStruct((t, n), BF16),
        grid=(t // bm,),
        in_specs=[pl.BlockSpec((bm, MLA_Q_LORA), lambda i: (i, cq_col // MLA_Q_LORA)),
                  pl.BlockSpec((1, MLA_Q_LORA), lambda i: (0, 0)),
                  pl.BlockSpec((MLA_Q_LORA, n), lambda i: (0, 0)),
                  tab_spec, tab_spec, tab_spec],
        out_specs=pl.BlockSpec((bm, n), lambda i: (i, 0)),
        compiler_params=_params(1),
        name="mla_q",
    )(proj, g.reshape(1, MLA_Q_LORA), w, *tabs)


def _mla_kv_kernel(ckv_ref, kr_ref, g_ref, w_ref, c_ref, s1_ref, s2_ref, k_ref, v_ref):
    h = _rms(ckv_ref[...].astype(F32), g_ref[...]).astype(BF16)
    y = _dot(h, w_ref[...])
    rot = _rope_group(kr_ref[...].astype(F32), c_ref[...], s1_ref[...], s2_ref[...])
    rot = rot.astype(k_ref.dtype)
    kw = MLA_HEADS * MLA_NOPE
    for hd in range(MLA_HEADS):
        lo = hd * MLA_QK_PAD
        k_ref[:, lo:lo + LANES] = y[:, hd * MLA_NOPE:(hd + 1) * MLA_NOPE].astype(k_ref.dtype)
        k_ref[:, lo + LANES:lo + MLA_QK_PAD] = rot
    v_ref[...] = y[:, kw:].astype(v_ref.dtype)


def _mla_kv(proj, ckv_col, kr_col, g, w, tabs, bm, seq):
    t = proj.shape[0]
    nsb = seq // bm
    tab_spec = pl.BlockSpec((bm, LANES), lambda i: (i % nsb, 0))
    return pl.pallas_call(
        _mla_kv_kernel,
        out_shape=(jax.ShapeDtypeStruct((t, MLA_HEADS * MLA_QK_PAD), BF16),
                   jax.ShapeDtypeStruct((t, MLA_HEADS * MLA_V), BF16)),
        grid=(t // bm,),
        in_specs=[pl.BlockSpec((bm, MLA_KV_LORA), lambda i: (i, ckv_col // MLA_KV_LORA)),
                  pl.BlockSpec((bm, LANES), lambda i: (i, kr_col // LANES)),
                  pl.BlockSpec((1, MLA_KV_LORA), lambda i: (0, 0)),
                  pl.BlockSpec(w.shape, lambda i: (0, 0)),
                  tab_spec, tab_spec, tab_spec],
        out_specs=(pl.BlockSpec((bm, MLA_HEADS * MLA_QK_PAD), lambda i: (i, 0)),
                   pl.BlockSpec((bm, MLA_HEADS * MLA_V), lambda i: (i, 0))),
        compiler_params=_params(1),
        name="mla_kv",
    )(proj, proj, g.reshape(1, MLA_KV_LORA), w, *tabs)


def _sb_attn_kernel(q_ref, k_ref, v_ref, o_ref, run_ref, acc_ref, *, bq, bk):
    i = pl.program_id(2)
    n_diag = bq // bk
    row = lax.broadcasted_iota(jnp.int32, (bk, bk), 0)
    col = lax.broadcasted_iota(jnp.int32, (bk, bk), 1)
    later = (row > col).astype(BF16)
    later2 = jnp.concatenate([later, later], axis=0)

    run_ref[...] = jnp.zeros(run_ref.shape, F32)
    acc_ref[...] = jnp.zeros(acc_ref.shape, F32)

    def tile(kb, diag):
        rows = slice(0 if diag is None else diag * bk, bq)
        start = pl.multiple_of(kb * bk, bk)
        k = k_ref[pl.ds(start, bk), :]
        v = v_ref[pl.ds(start, bk), :]
        z = _dot_nt(q_ref[rows, :], k)
        z_pos = jnp.maximum(z, 0.0)
        z_neg = z - z_pos
        log_term = jnp.log(1.0 + jnp.exp2(z_neg - z_pos)) * LOG2E
        sp = z_pos + log_term
        if diag is not None:
            causal = (lax.broadcasted_iota(jnp.int32, z.shape, 1)
                      < lax.broadcasted_iota(jnp.int32, z.shape, 0))
            sp = jnp.where(causal, sp, 0.0)
        hi = sp.astype(BF16)
        split = jnp.concatenate([hi, (sp - hi.astype(F32)).astype(BF16)], axis=1)
        suffix = _dot(split, later2)
        a = jnp.exp2(z_neg - log_term - suffix)
        if diag is not None:
            a = jnp.where(causal, a, 0.0)
        run = run_ref[rows, :]
        acc_ref[rows, :] += jnp.exp2(-run) * _dot(a.astype(BF16), v)
        run_ref[rows, :] = run + (suffix[:, 0:1] + sp[:, 0:1])

    for d in range(n_diag - 1, -1, -1):
        tile(i * n_diag + d, d)
    n_before = i * n_diag

    def body(t, carry):
        tile(n_before - 1 - t, None)
        return carry

    lax.fori_loop(0, n_before, body, 0)
    o_ref[...] = acc_ref[...].astype(o_ref.dtype)


def _sb_attn(proj, batch, seq, bq, bk):
    t = proj.shape[0]
    nq = seq // bq
    qc, kc, vc = (c // SB_HEAD_DIM for c in (_IN_SBQ, _IN_SBK, _IN_SBV))
    return pl.pallas_call(
        functools.partial(_sb_attn_kernel, bq=bq, bk=bk),
        out_shape=jax.ShapeDtypeStruct((t, _SBW), BF16),
        grid=(batch, SB_HEADS, nq),
        in_specs=[pl.BlockSpec((bq, SB_HEAD_DIM), lambda b, h, i: (b * nq + i, qc + h)),
                  pl.BlockSpec((seq, SB_HEAD_DIM), lambda b, h, i: (b, kc + h)),
                  pl.BlockSpec((seq, SB_HEAD_DIM), lambda b, h, i: (b, vc + h))],
        out_specs=pl.BlockSpec((bq, SB_HEAD_DIM), lambda b, h, i: (b * nq + i, h)),
        scratch_shapes=[pltpu.VMEM((bq, LANES), F32), pltpu.VMEM((bq, SB_HEAD_DIM), F32)],
        compiler_params=_params(3),
        name="sb_attn",
    )(proj, proj, proj)


def _mla_attn_kernel(q_ref, k_ref, v_ref, o_ref, m_ref, acc_ref, *, bq, bk, sub):
    i = pl.program_id(2)
    n_diag = bq // bk
    m_ref[...] = jnp.full(m_ref.shape, -jnp.inf, F32)
    acc_ref[...] = jnp.zeros(acc_ref.shape, F32)
    ones = jnp.ones((bk, LANES), BF16)

    def tile(kb, diag):
        start = pl.multiple_of(kb * bk, bk)
        k = k_ref[pl.ds(start, bk), :]
        v1 = jnp.concatenate([v_ref[pl.ds(start, bk), :], ones], axis=1)
        for s in range(bq // sub):
            rows = slice(s * sub, (s + 1) * sub)
            if diag is not None and (s + 1) * sub <= diag * bk:
                continue
            sc = _dot_nt(q_ref[rows, :], k)
            if diag is not None and s * sub < (diag + 1) * bk:
                qchunk = (lax.broadcasted_iota(jnp.int32, (sub, bk), 0) + s * sub) // CHUNK
                kchunk = (lax.broadcasted_iota(jnp.int32, (sub, bk), 1) + diag * bk) // CHUNK
                sc = jnp.where(kchunk <= qchunk, sc, -jnp.inf)
            m = m_ref[rows, :]
            m_new = jnp.maximum(m, jnp.max(sc, axis=1, keepdims=True))
            alpha = jnp.exp2(m - m_new)
            p = jnp.exp2(sc - jnp.tile(m_new, (1, bk // LANES)))
            acc_ref[rows, :] = (jnp.tile(alpha, (1, 2)) * acc_ref[rows, :]
                                + _dot(p.astype(BF16), v1))
            m_ref[rows, :] = m_new

    for d in range(n_diag):
        tile(i * n_diag + d, d)
    n_before = i * n_diag

    def body(t, carry):
        tile(n_before - 1 - t, None)
        return carry

    lax.fori_loop(0, n_before, body, 0)
    o_ref[...] = (acc_ref[:, 0:MLA_V] / acc_ref[:, MLA_V:2 * MLA_V]).astype(o_ref.dtype)


def _mla_attn(q, k, v, batch, seq, bq, bk, sub):
    t = q.shape[0]
    nq = seq // bq
    return pl.pallas_call(
        functools.partial(_mla_attn_kernel, bq=bq, bk=bk, sub=sub),
        out_shape=jax.ShapeDtypeStruct((t, MLA_HEADS * MLA_V), BF16),
        grid=(batch, MLA_HEADS, nq),
        in_specs=[pl.BlockSpec((bq, MLA_QK_PAD), lambda b, h, i: (b * nq + i, h)),
                  pl.BlockSpec((seq, MLA_QK_PAD), lambda b, h, i: (b, h)),
                  pl.BlockSpec((seq, MLA_V), lambda b, h, i: (b, h))],
        out_specs=pl.BlockSpec((bq, MLA_V), lambda b, h, i: (b * nq + i, h)),
        scratch_shapes=[pltpu.VMEM((bq, LANES), F32), pltpu.VMEM((bq, 2 * MLA_V), F32)],
        compiler_params=_params(3),
        name="mla_attn",
    )(q, k, v)


def _merge_kernel(osb_ref, omla_ref, wsb_ref, wmla_ref, gsb_ref, gmla_ref, bsb_ref, bmla_ref,
                  o_ref):
    gate_sb = jax.nn.sigmoid(gsb_ref[...].astype(F32) + bsb_ref[...])
    gate_mla = jax.nn.sigmoid(gmla_ref[...].astype(F32) + bmla_ref[...])
    mixed = (gate_sb * _dot(osb_ref[...], wsb_ref[...])
             + gate_mla * _dot(omla_ref[...], wmla_ref[...]))
    o_ref[...] = mixed.astype(o_ref.dtype)


def _merge(o_sb, o_mla, w_sb, w_mla, layer, proj, gsb_col, gmla_col, b_gate, bm, bn):
    t, k = o_sb.shape
    d = w_sb.shape[2]
    nj = d // bn
    return pl.pallas_call(
        _merge_kernel,
        out_shape=jax.ShapeDtypeStruct((t, d), BF16),
        grid=(t // bm, nj),
        in_specs=[pl.BlockSpec((bm, k), lambda i, j: (i, 0)),
                  pl.BlockSpec((bm, k), lambda i, j: (i, 0)),
                  pl.BlockSpec((None, k, bn), lambda i, j: (layer, 0, j)),
                  pl.BlockSpec((None, k, bn), lambda i, j: (layer, 0, j)),
                  pl.BlockSpec((bm, bn), lambda i, j: (i, gsb_col // bn + j)),
                  pl.BlockSpec((bm, bn), lambda i, j: (i, gmla_col // bn + j)),
                  pl.BlockSpec((1, bn), lambda i, j: (0, j)),
                  pl.BlockSpec((1, bn), lambda i, j: (0, nj + j))],
        out_specs=pl.BlockSpec((bm, bn), lambda i, j: (i, j)),
        compiler_params=_params(2),
        name="merge",
    )(o_sb, o_mla, w_sb, w_mla, proj, proj, b_gate, b_gate)


def _residual_norm_kernel(a_ref, w_ref, x_ref, g_ref, o_ref, h_ref):
    out = x_ref[...] + _dot(a_ref[...], w_ref[...])
    o_ref[...] = out
    h_ref[...] = _rms(out, g_ref[...]).astype(h_ref.dtype)


def _residual_norm(a, w, layer, x, g, bm):
    t, k = a.shape
    d = w.shape[2]
    return pl.pallas_call(
        _residual_norm_kernel,
        out_shape=(jax.ShapeDtypeStruct((t, d), F32), jax.ShapeDtypeStruct((t, d), BF16)),
        grid=(t // bm,),
        in_specs=[pl.BlockSpec((bm, k), lambda i: (i, 0)),
                  pl.BlockSpec((None, k, d), lambda i: (layer, 0, 0)),
                  pl.BlockSpec((bm, d), lambda i: (i, 0)),
                  pl.BlockSpec((1, d), lambda i: (0, 0))],
        out_specs=(pl.BlockSpec((bm, d), lambda i: (i, 0)),
                   pl.BlockSpec((bm, d), lambda i: (i, 0))),
        compiler_params=_params(1),
        name="residual_norm",
    )(a, w, x, g.reshape(1, d))


def _residual_kernel(a_ref, w_ref, x_ref, o_ref):
    o_ref[...] = x_ref[...] + _dot(a_ref[...], w_ref[...])


def _residual(a, w, layer, x, bm, bn):
    t, k = a.shape
    d = w.shape[2]
    return pl.pallas_call(
        _residual_kernel,
        out_shape=jax.ShapeDtypeStruct((t, d), F32),
        grid=(d // bn, t // bm),
        in_specs=[pl.BlockSpec((bm, k), lambda j, i: (i, 0)),
                  pl.BlockSpec((None, k, bn), lambda j, i: (layer, 0, j)),
                  pl.BlockSpec((bm, bn), lambda j, i: (i, j))],
        out_specs=pl.BlockSpec((bm, bn), lambda j, i: (i, j)),
        compiler_params=_params(2),
        name="ffn_down",
    )(a, w, x)


def _ffn_up_kernel(h_ref, wg_ref, wv_ref, cwg_ref, cwv_ref, cbg_ref, cbv_ref, o_ref,
                   sg_ref, sv_ref, wgb_ref, wvb_ref, *, bm, chunk, blocks_per_seq):
    i = pl.program_id(1)

    @pl.when(i == 0)
    def _():
        wgb_ref[...] = wg_ref[...].astype(wgb_ref.dtype)
        wvb_ref[...] = wv_ref[...].astype(wvb_ref.dtype)

    starts_sequence = (i % blocks_per_seq) == 0
    for s_ref in (sg_ref, sv_ref):
        s_ref[0:SUBLANES, :] = jnp.where(starts_sequence, 0.0, s_ref[bm:bm + SUBLANES, :])

    def conv(s_ref, cw_ref, cb_ref, cols):
        lo = SUBLANES
        acc = cb_ref[:, cols] + cw_ref[0:1, cols] * s_ref[lo - 2:lo - 2 + bm, cols]
        acc = acc + cw_ref[1:2, cols] * s_ref[lo - 1:lo - 1 + bm, cols]
        return acc + cw_ref[2:3, cols] * s_ref[lo:lo + bm, cols]

    h = h_ref[...]
    for c in range(o_ref.shape[1] // chunk):
        cols = slice(c * chunk, (c + 1) * chunk)
        sg_ref[SUBLANES:SUBLANES + bm, cols] = _dot(h, wgb_ref[:, cols])
        sv_ref[SUBLANES:SUBLANES + bm, cols] = _dot(h, wvb_ref[:, cols])
        gate = conv(sg_ref, cwg_ref, cbg_ref, cols)
        val = conv(sv_ref, cwv_ref, cbv_ref, cols)
        o_ref[:, cols] = (gate * jax.nn.sigmoid(gate) * val).astype(o_ref.dtype)


def _ffn_up(h, w_up, layer, conv_w, conv_b, seq, bm, bn, chunk):
    t, d = h.shape
    d_ff = w_up.shape[2] // 2
    nj = d_ff // bn
    return pl.pallas_call(
        functools.partial(_ffn_up_kernel, bm=bm, chunk=chunk, blocks_per_seq=seq // bm),
        out_shape=jax.ShapeDtypeStruct((t, d_ff), BF16),
        grid=(nj, t // bm),
        in_specs=[pl.BlockSpec((bm, d), lambda j, i: (i, 0)),
                  pl.BlockSpec((None, d, bn), lambda j, i: (layer, 0, j)),
                  pl.BlockSpec((None, d, bn), lambda j, i: (layer, 0, nj + j)),
                  pl.BlockSpec((CONV_WIDTH, bn), lambda j, i: (0, j)),
                  pl.BlockSpec((CONV_WIDTH, bn), lambda j, i: (0, nj + j)),
                  pl.BlockSpec((1, bn), lambda j, i: (0, j)),
                  pl.BlockSpec((1, bn), lambda j, i: (0, nj + j))],
        out_specs=pl.BlockSpec((bm, bn), lambda j, i: (i, j)),
        scratch_shapes=[pltpu.VMEM((bm + SUBLANES, bn), F32),
                        pltpu.VMEM((bm + SUBLANES, bn), F32),
                        pltpu.VMEM((d, bn), BF16), pltpu.VMEM((d, bn), BF16)],
        compiler_params=_params(2),
        name="ffn_up",
    )(h, w_up, w_up, conv_w, conv_w, conv_b, conv_b)


def _rope_tables(seq):
    half = MLA_ROPE // 2
    inv = 1.0 / (ROPE_THETA ** (jnp.arange(0, MLA_ROPE, 2, dtype=F32) / MLA_ROPE))
    ang = jnp.arange(seq, dtype=F32)[:, None] * inv[None, :]
    cos, sin = jnp.cos(ang), jnp.sin(ang)
    zero = jnp.zeros((seq, half), F32)
    tail = jnp.zeros((seq, LANES - MLA_ROPE), F32)
    c = jnp.concatenate([cos, cos, tail], axis=1)
    s1 = jnp.concatenate([-sin, zero, tail], axis=1)
    s2 = jnp.concatenate([zero, sin, tail], axis=1)
    return c, s1, s2


def _block(n, want):
    b = min(n, want)
    assert n % b == 0, (n, want)
    return b


def _in_layout(d_model):
    src = {"sbq": 0, "sbk": _SBW, "sbv": 2 * _SBW, "cq": 3 * _SBW}
    src["ckv"] = src["cq"] + MLA_Q_LORA
    src["krope"] = src["ckv"] + MLA_KV_LORA
    src["gsb"] = src["krope"] + MLA_ROPE
    src["gmla"] = src["gsb"] + d_model
    src["end"] = src["gmla"] + d_model
    dst = {"sbq": _IN_SBQ, "sbk": _IN_SBK, "sbv": _IN_SBV, "gsb": _IN_GSB,
           "gmla": _IN_GSB + d_model}
    dst["ckv"] = dst["gmla"] + d_model
    dst["cq"] = dst["ckv"] + MLA_KV_LORA
    dst["krope"] = dst["cq"] + MLA_Q_LORA
    dst["end"] = -(-(dst["krope"] + LANES) // _IN_BLOCK) * _IN_BLOCK
    assert dst["ckv"] % MLA_KV_LORA == 0 and dst["cq"] % MLA_Q_LORA == 0
    assert dst["krope"] % LANES == 0 and dst["gmla"] % _IN_BLOCK == 0
    return src, dst


def _prep_w_in_starts(d_model):
    src, dst = _in_layout(d_model)
    widths = {"sbq": _SBW, "sbk": _SBW, "sbv": _SBW, "gsb": d_model, "gmla": d_model,
              "ckv": MLA_KV_LORA, "cq": MLA_Q_LORA}
    starts = []
    for b in range(dst["end"] // _IN_BLOCK):
        r = b * _IN_BLOCK
        name = max((n for n in widths if dst[n] <= r), key=lambda n: dst[n])
        assert r + _IN_BLOCK <= dst[name] + widths[name] or b == dst["end"] // _IN_BLOCK - 1
        starts.append(src[name] + r - dst[name])
    return tuple(starts)


def _prep_w_in_kernel(a_ref, kr_ref, o_ref, *, d_model):
    _, dst = _in_layout(d_model)
    b = pl.program_id(1)
    n_blocks = dst["end"] // _IN_BLOCK
    n_scaled = _SBW // _IN_BLOCK
    cq_rows = dst["krope"] - (n_blocks - 1) * _IN_BLOCK

    @pl.when(b < n_scaled)
    def _():
        o_ref[...] = (a_ref[0] * (SB_HEAD_DIM ** -0.5 * LOG2E)).astype(o_ref.dtype)

    @pl.when((b >= n_scaled) & (b < n_blocks - 1))
    def _():
        o_ref[...] = a_ref[0].astype(o_ref.dtype)

    @pl.when(b == n_blocks - 1)
    def _():
        o_ref[0:cq_rows, :] = a_ref[0, 0:cq_rows, :].astype(o_ref.dtype)
        o_ref[cq_rows:cq_rows + MLA_ROPE, :] = kr_ref[0].astype(o_ref.dtype)
        o_ref[cq_rows + MLA_ROPE:, :] = jnp.zeros(
            (_IN_BLOCK - cq_rows - MLA_ROPE, o_ref.shape[1]), o_ref.dtype)


def _prep_w_in(w_in):
    depth, k, n = w_in.shape
    src, dst = _in_layout(k)
    assert src["end"] == n
    starts = _prep_w_in_starts(k)
    w_t = jnp.swapaxes(w_in, 1, 2)

    def a_map(l, b):
        start = jnp.int32(starts[-1])
        for blk in range(len(starts) - 2, -1, -1):
            start = jnp.where(b == blk, starts[blk], start)
        return (l, pl.multiple_of(start, MLA_ROPE), 0)

    return pl.pallas_call(
        functools.partial(_prep_w_in_kernel, d_model=k),
        out_shape=jax.ShapeDtypeStruct((depth, dst["end"], k), BF16),
        grid=(depth, len(starts)),
        in_specs=[pl.BlockSpec((pl.Element(1), pl.Element(_IN_BLOCK), pl.Element(k)), a_map),
                  pl.BlockSpec((pl.Element(1), pl.Element(MLA_ROPE), pl.Element(k)),
                               lambda l, b: (l, src["krope"], 0))],
        out_specs=pl.BlockSpec((None, _IN_BLOCK, k), lambda l, b: (l, b, 0)),
        compiler_params=_params(2),
        name="prep_w_in",
    )(w_t, w_t)


def _cast_kernel(w_ref, o_ref):
    o_ref[...] = w_ref[...].astype(o_ref.dtype)


def _cast_bf16(w, rows):
    depth, k, n = w.shape
    return pl.pallas_call(
        _cast_kernel,
        out_shape=jax.ShapeDtypeStruct(w.shape, BF16),
        grid=(depth, k // rows),
        in_specs=[pl.BlockSpec((None, rows, n), lambda l, i: (l, i, 0))],
        out_specs=pl.BlockSpec((None, rows, n), lambda l, i: (l, i, 0)),
        compiler_params=_params(2),
        name="cast_bf16",
    )(w)


def _prep_w_uq(w_uq):
    per = MLA_NOPE + MLA_ROPE
    w = w_uq.reshape(w_uq.shape[0], MLA_HEADS, per) * (per ** -0.5 * LOG2E)
    w = jnp.pad(w, ((0, 0), (0, 0), (0, MLA_QK_PAD - per)))
    return w.reshape(w_uq.shape[0], MLA_HEADS * MLA_QK_PAD).astype(BF16)


def _prep_w_ukv(w_ukv):
    w = w_ukv.reshape(w_ukv.shape[0], MLA_HEADS, MLA_NOPE + MLA_V)
    k = w[:, :, :MLA_NOPE].reshape(w_ukv.shape[0], MLA_HEADS * MLA_NOPE)
    v = w[:, :, MLA_NOPE:].reshape(w_ukv.shape[0], MLA_HEADS * MLA_V)
    return jnp.concatenate([k, v], axis=1).astype(BF16)


def kernel(x, norm1_g, w_in, b_gate, q_norm_g, w_uq, kv_norm_g, w_ukv, w_proj_sb, w_proj_mla,
           w_out, norm2_g, w_up, conv_w, conv_b, w_down, final_g):
    batch, seq, d_model = x.shape
    depth = w_in.shape[0]
    t = batch * seq
    tabs = _rope_tables(seq)

    bm_big = _block(seq, 1024)
    bm_mid = _block(seq, 512)
    sb_bq, sb_bk = _block(seq, 2048), _block(seq, 256)
    mla_bq = _block(seq, 4096)
    mla_bk, mla_sub = _block(mla_bq, 512), _block(mla_bq, 256)

    _, cols = _in_layout(d_model)
    w_in_b = _prep_w_in(w_in)
    w_sb_b, w_mla_b = _cast_bf16(w_proj_sb, 512), _cast_bf16(w_proj_mla, 512)
    w_out_b, w_down_b = _cast_bf16(w_out, 512), _cast_bf16(w_down, 512)

    xf = x.reshape(t, d_model)
    for l in range(depth):
        proj = _norm_matmul(xf, norm1_g[l], w_in_b, l, bm_big, _IN_BLOCK)
        q_mla = _mla_q(proj, cols["cq"], q_norm_g[l], _prep_w_uq(w_uq[l]), tabs, bm_mid, seq)
        k_mla, v_mla = _mla_kv(proj, cols["ckv"], cols["krope"], kv_norm_g[l],
                               _prep_w_ukv(w_ukv[l]), tabs, bm_mid, seq)
        o_sb = _sb_attn(proj, batch, seq, sb_bq, sb_bk)
        o_mla = _mla_attn(q_mla, k_mla, v_mla, batch, seq, mla_bq, mla_bk, mla_sub)
        mixed = _merge(o_sb, o_mla, w_sb_b, w_mla_b, l, proj, cols["gsb"], cols["gmla"],
                       b_gate[l].reshape(1, -1), bm_big, 2 * _IN_BLOCK)
        x1, h2 = _residual_norm(mixed, w_out_b, l, xf, norm2_g[l], bm_mid)
        act = _ffn_up(h2, w_up, l, conv_w[l], conv_b[l].reshape(1, -1), seq,
                      bm_big, 512, 256)
        xf = _residual(act, w_down_b, l, x1, bm_mid, 1024)
    return _rmsnorm(xf, final_g, bm_mid).reshape(batch, seq, d_model)
```

```python
import functools
import math

import jax
import jax.numpy as jnp
from jax import lax
from jax.experimental import pallas as pl
from jax.experimental.pallas import tpu as pltpu

F32 = jnp.float32
BF16 = jnp.bfloat16

EPS = 1e-6
LOG2E = math.log2(math.e)
ROPE_THETA = 10000.0
CHUNK = 64
SB_HEADS = 8
SB_HEAD_DIM = 128
MLA_HEADS = 8
MLA_NOPE = 128
MLA_ROPE = 64
MLA_V = 128
MLA_Q_LORA = 768
MLA_KV_LORA = 512
CONV_WIDTH = 3

LANES = 128
SUBLANES = 8
MLA_QK_PAD = 2 * LANES
VMEM_LIMIT_BYTES = 56 * 1024 * 1024

_SBW = SB_HEADS * SB_HEAD_DIM
_IN_SBQ = 0
_IN_SBK = _SBW
_IN_SBV = 2 * _SBW
_IN_GSB = 3 * _SBW
_IN_BLOCK = 512


def _params(n_axes):
    return pltpu.CompilerParams(
        dimension_semantics=("arbitrary",) * n_axes, vmem_limit_bytes=VMEM_LIMIT_BYTES)


def _rms(x, g):
    return x * lax.rsqrt(jnp.mean(x * x, axis=-1, keepdims=True) + EPS) * g


def _rope_group(g, c, s1, s2):
    return g * c + pltpu.roll(g, 96, 1) * s1 + pltpu.roll(g, 32, 1) * s2


def _dot(a, b):
    return jnp.dot(a, b, preferred_element_type=F32)


def _dot_nt(a, b):
    return lax.dot_general(a, b, (((1,), (1,)), ((), ())), preferred_element_type=F32)


def _rmsnorm_kernel(x_ref, g_ref, o_ref):
    o_ref[...] = _rms(x_ref[...], g_ref[...]).astype(o_ref.dtype)


def _rmsnorm(x, g, bm):
    t, d = x.shape
    return pl.pallas_call(
        _rmsnorm_kernel,
        out_shape=jax.ShapeDtypeStruct((t, d), x.dtype),
        grid=(t // bm,),
        in_specs=[pl.BlockSpec((bm, d), lambda i: (i, 0)),
                  pl.BlockSpec((1, d), lambda i: (0, 0))],
        out_specs=pl.BlockSpec((bm, d), lambda i: (i, 0)),
        compiler_params=_params(1),
        name="rmsnorm",
    )(x, g.reshape(1, d))


def _norm_matmul_kernel(x_ref, g_ref, w_ref, o_ref, h_ref):
    @pl.when(pl.program_id(1) == 0)
    def _():
        h_ref[...] = _rms(x_ref[...], g_ref[...]).astype(h_ref.dtype)

    o_ref[...] = _dot_nt(h_ref[...], w_ref[...]).astype(o_ref.dtype)


def _norm_matmul(x, g, w_t, layer, bm, bn):
    t, k = x.shape
    n = w_t.shape[1]
    return pl.pallas_call(
        _norm_matmul_kernel,
        out_shape=jax.ShapeDtypeStruct((t, n), BF16),
        grid=(t // bm, n // bn),
        in_specs=[pl.BlockSpec((bm, k), lambda i, j: (i, 0)),
                  pl.BlockSpec((1, k), lambda i, j: (0, 0)),
                  pl.BlockSpec((None, bn, k), lambda i, j: (layer, j, 0))],
        out_specs=pl.BlockSpec((bm, bn), lambda i, j: (i, j)),
        scratch_shapes=[pltpu.VMEM((bm, k), BF16)],
        compiler_params=_params(2),
        name="in_proj",
    )(x, g.reshape(1, k), w_t)


def _mla_q_kernel(cq_ref, g_ref, w_ref, c_ref, s1_ref, s2_ref, o_ref):
    h = _rms(cq_ref[...].astype(F32), g_ref[...]).astype(BF16)
    y = _dot(h, w_ref[...])
    c, s1, s2 = c_ref[...], s1_ref[...], s2_ref[...]
    for hd in range(MLA_HEADS):
        lo = hd * MLA_QK_PAD
        o_ref[:, lo:lo + LANES] = y[:, lo:lo + LANES].astype(o_ref.dtype)
        rot = _rope_group(y[:, lo + LANES:lo + MLA_QK_PAD], c, s1, s2)
        o_ref[:, lo + LANES:lo + MLA_QK_PAD] = rot.astype(o_ref.dtype)


def _mla_q(proj, cq_col, g, w, tabs, bm, seq):
    t = proj.shape[0]
    n = w.shape[1]
    nsb = seq // bm
    tab_spec = pl.BlockSpec((bm, LANES), lambda i: (i % nsb, 0))
    return pl.pallas_call(
        _mla_q_kernel,
        out_shape=jax.ShapeDtypeStruct((t, n), BF16),
        grid=(t // bm,),
        in_specs=[pl.BlockSpec((bm, MLA_Q_LORA), lambda i: (i, cq_col // MLA_Q_LORA)),
                  pl.BlockSpec((1, MLA_Q_LORA), lambda i: (0, 0)),
                  pl.BlockSpec((MLA_Q_LORA, n), lambda i: (0, 0)),
                  tab_spec, tab_spec, tab_spec],
        out_specs=pl.BlockSpec((bm, n), lambda i: (i, 0)),
        compiler_params=_params(1),
        name="mla_q",
    )(proj, g.reshape(1, MLA_Q_LORA), w, *tabs)


def _mla_kv_kernel(ckv_ref, kr_ref, g_ref, w_ref, c_ref, s1_ref, s2_ref, k_ref, v_ref):
    h = _rms(ckv_ref[...].astype(F32), g_ref[...]).astype(BF16)
    y = _dot(h, w_ref[...])
    rot = _rope_group(kr_ref[...].astype(F32), c_ref[...], s1_ref[...], s2_ref[...])
    rot = rot.astype(k_ref.dtype)
    kw = MLA_HEADS * MLA_NOPE
    for hd in range(MLA_HEADS):
        lo = hd * MLA_QK_PAD
        k_ref[:, lo:lo + LANES] = y[:, hd * MLA_NOPE:(hd + 1) * MLA_NOPE].astype(k_ref.dtype)
        k_ref[:, lo + LANES:lo + MLA_QK_PAD] = rot
    v_ref[...] = y[:, kw:].astype(v_ref.dtype)


def _mla_kv(proj, ckv_col, kr_col, g, w, tabs, bm, seq):
    t = proj.shape[0]
    nsb = seq // bm
    tab_spec = pl.BlockSpec((bm, LANES), lambda i: (i % nsb, 0))
    return pl.pallas_call(
        _mla_kv_kernel,
        out_shape=(jax.ShapeDtypeStruct((t, MLA_HEADS * MLA_QK_PAD), BF16),
                   jax.ShapeDtypeStruct((t, MLA_HEADS * MLA_V), BF16)),
        grid=(t // bm,),
        in_specs=[pl.BlockSpec((bm, MLA_KV_LORA), lambda i: (i, ckv_col // MLA_KV_LORA)),
                  pl.BlockSpec((bm, LANES), lambda i: (i, kr_col // LANES)),
                  pl.BlockSpec((1, MLA_KV_LORA), lambda i: (0, 0)),
                  pl.BlockSpec(w.shape, lambda i: (0, 0)),
                  tab_spec, tab_spec, tab_spec],
        out_specs=(pl.BlockSpec((bm, MLA_HEADS * MLA_QK_PAD), lambda i: (i, 0)),
                   pl.BlockSpec((bm, MLA_HEADS * MLA_V), lambda i: (i, 0))),
        compiler_params=_params(1),
        name="mla_kv",
    )(proj, proj, g.reshape(1, MLA_KV_LORA), w, *tabs)


def _sb_attn_kernel(q_ref, k_ref, v_ref, o_ref, run_ref, acc_ref, *, bq, bk):
    i = pl.program_id(2)
    n_diag = bq // bk
    row = lax.broadcasted_iota(jnp.int32, (bk, bk), 0)
    col = lax.broadcasted_iota(jnp.int32, (bk, bk), 1)
    later = (row > col).astype(BF16)
    later2 = jnp.concatenate([later, later], axis=0)

    run_ref[...] = jnp.zeros(run_ref.shape, F32)
    acc_ref[...] = jnp.zeros(acc_ref.shape, F32)

    def tile(kb, diag):
        rows = slice(0 if diag is None else diag * bk, bq)
        start = pl.multiple_of(kb * bk, bk)
        k = k_ref[pl.ds(start, bk), :]
        v = v_ref[pl.ds(start, bk), :]
        z = _dot_nt(q_ref[rows, :], k)
        z_pos = jnp.maximum(z, 0.0)
        z_neg = z - z_pos
        log_term = jnp.log(1.0 + jnp.exp2(z_neg - z_pos)) * LOG2E
        sp = z_pos + log_term
        if diag is not None:
            causal = (lax.broadcasted_iota(jnp.int32, z.shape, 1)
                      < lax.broadcasted_iota(jnp.int32, z.shape, 0))
            sp = jnp.where(causal, sp, 0.0)
        hi = sp.astype(BF16)
        split = jnp.concatenate([hi, (sp - hi.astype(F32)).astype(BF16)], axis=1)
        suffix = _dot(split, later2)
        a = jnp.exp2(z_neg - log_term - suffix)
        if diag is not None:
            a = jnp.where(causal, a, 0.0)
        run = run_ref[rows, :]
        acc_ref[rows, :] += jnp.exp2(-run) * _dot(a.astype(BF16), v)
        run_ref[rows, :] = run + (suffix[:, 0:1] + sp[:, 0:1])

    for d in range(n_diag - 1, -1, -1):
        tile(i * n_diag + d, d)
    n_before = i * n_diag

    def body(t, carry):
        tile(n_before - 1 - t, None)
        return carry

    lax.fori_loop(0, n_before, body, 0)
    o_ref[...] = acc_ref[...].astype(o_ref.dtype)


def _sb_attn(proj, batch, seq, bq, bk):
    t = proj.shape[0]
    nq = seq // bq
    qc, kc, vc = (c // SB_HEAD_DIM for c in (_IN_SBQ, _IN_SBK, _IN_SBV))
    return pl.pallas_call(
        functools.partial(_sb_attn_kernel, bq=bq, bk=bk),
        out_shape=jax.ShapeDtypeStruct((t, _SBW), BF16),
        grid=(batch, SB_HEADS, nq),
        in_specs=[pl.BlockSpec((bq, SB_HEAD_DIM), lambda b, h, i: (b * nq + i, qc + h)),
                  pl.BlockSpec((seq, SB_HEAD_DIM), lambda b, h, i: (b, kc + h)),
                  pl.BlockSpec((seq, SB_HEAD_DIM), lambda b, h, i: (b, vc + h))],
        out_specs=pl.BlockSpec((bq, SB_HEAD_DIM), lambda b, h, i: (b * nq + i, h)),
        scratch_shapes=[pltpu.VMEM((bq, LANES), F32), pltpu.VMEM((bq, SB_HEAD_DIM), F32)],
        compiler_params=_params(3),
        name="sb_attn",
    )(proj, proj, proj)


def _mla_attn_kernel(q_ref, k_ref, v_ref, o_ref, m_ref, acc_ref, *, bq, bk, sub):
    i = pl.program_id(2)
    n_diag = bq // bk
    m_ref[...] = jnp.full(m_ref.shape, -jnp.inf, F32)
    acc_ref[...] = jnp.zeros(acc_ref.shape, F32)
    ones = jnp.ones((bk, LANES), BF16)

    def tile(kb, diag):
        start = pl.multiple_of(kb * bk, bk)
        k = k_ref[pl.ds(start, bk), :]
        v1 = jnp.concatenate([v_ref[pl.ds(start, bk), :], ones], axis=1)
        for s in range(bq // sub):
            rows = slice(s * sub, (s + 1) * sub)
            if diag is not None and (s + 1) * sub <= diag * bk:
                continue
            sc = _dot_nt(q_ref[rows, :], k)
            if diag is not None and s * sub < (diag + 1) * bk:
                qchunk = (lax.broadcasted_iota(jnp.int32, (sub, bk), 0) + s * sub) // CHUNK
                kchunk = (lax.broadcasted_iota(jnp.int32, (sub, bk), 1) + diag * bk) // CHUNK
                sc = jnp.where(kchunk <= qchunk, sc, -jnp.inf)
            m = m_ref[rows, :]
            m_new = jnp.maximum(m, jnp.max(sc, axis=1, keepdims=True))
            alpha = jnp.exp2(m - m_new)
            p = jnp.exp2(sc - jnp.tile(m_new, (1, bk // LANES)))
            acc_ref[rows, :] = (jnp.tile(alpha, (1, 2)) * acc_ref[rows, :]
                                + _dot(p.astype(BF16), v1))
            m_ref[rows, :] = m_new

    for d in range(n_diag):
        tile(i * n_diag + d, d)
    n_before = i * n_diag

    def body(t, carry):
        tile(n_before - 1 - t, None)
        return carry

    lax.fori_loop(0, n_before, body, 0)
    o_ref[...] = (acc_ref[:, 0:MLA_V] / acc_ref[:, MLA_V:2 * MLA_V]).astype(o_ref.dtype)


def _mla_attn(q, k, v, batch, seq, bq, bk, sub):
    t = q.shape[0]
    nq = seq // bq
    return pl.pallas_call(
        functools.partial(_mla_attn_kernel, bq=bq, bk=bk, sub=sub),
        out_shape=jax.ShapeDtypeStruct((t, MLA_HEADS * MLA_V), BF16),
        grid=(batch, MLA_HEADS, nq),
        in_specs=[pl.BlockSpec((bq, MLA_QK_PAD), lambda b, h, i: (b * nq + i, h)),
                  pl.BlockSpec((seq, MLA_QK_PAD), lambda b, h, i: (b, h)),
                  pl.BlockSpec((seq, MLA_V), lambda b, h, i: (b, h))],
        out_specs=pl.BlockSpec((bq, MLA_V), lambda b, h, i: (b * nq + i, h)),
        scratch_shapes=[pltpu.VMEM((bq, LANES), F32), pltpu.VMEM((bq, 2 * MLA_V), F32)],
        compiler_params=_params(3),
        name="mla_attn",
    )(q, k, v)


def _merge_kernel(osb_ref, omla_ref, wsb_ref, wmla_ref, gsb_ref, gmla_ref, bsb_ref, bmla_ref,
                  o_ref):
    gate_sb = jax.nn.sigmoid(gsb_ref[...].astype(F32) + bsb_ref[...])
    gate_mla = jax.nn.sigmoid(gmla_ref[...].astype(F32) + bmla_ref[...])
    mixed = (gate_sb * _dot(osb_ref[...], wsb_ref[...])
             + gate_mla * _dot(omla_ref[...], wmla_ref[...]))
    o_ref[...] = mixed.astype(o_ref.dtype)


def _merge(o_sb, o_mla, w_sb, w_mla, layer, proj, gsb_col, gmla_col, b_gate, bm, bn):
    t, k = o_sb.shape
    d = w_sb.shape[2]
    nj = d // bn
    return pl.pallas_call(
        _merge_kernel,
        out_shape=jax.ShapeDtypeStruct((t, d), BF16),
        grid=(t // bm, nj),
        in_specs=[pl.BlockSpec((bm, k), lambda i, j: (i, 0)),
                  pl.BlockSpec((bm, k), lambda i, j: (i, 0)),
                  pl.BlockSpec((None, k, bn), lambda i, j: (layer, 0, j)),
                  pl.BlockSpec((None, k, bn), lambda i, j: (layer, 0, j)),
                  pl.BlockSpec((bm, bn), lambda i, j: (i, gsb_col // bn + j)),
                  pl.BlockSpec((bm, bn), lambda i, j: (i, gmla_col // bn + j)),
                  pl.BlockSpec((1, bn), lambda i, j: (0, j)),
                  pl.BlockSpec((1, bn), lambda i, j: (0, nj + j))],
        out_specs=pl.BlockSpec((bm, bn), lambda i, j: (i, j)),
        compiler_params=_params(2),
        name="merge",
    )(o_sb, o_mla, w_sb, w_mla, proj, proj, b_gate, b_gate)


def _residual_norm_kernel(a_ref, w_ref, x_ref, g_ref, o_ref, h_ref):
    out = x_ref[...] + _dot(a_ref[...], w_ref[...])
    o_ref[...] = out
    h_ref[...] = _rms(out, g_ref[...]).astype(h_ref.dtype)


def _residual_norm(a, w, layer, x, g, bm):
    t, k = a.shape
    d = w.shape[2]
    return pl.pallas_call(
        _residual_norm_kernel,
        out_shape=(jax.ShapeDtypeStruct((t, d), F32), jax.ShapeDtypeStruct((t, d), BF16)),
        grid=(t // bm,),
        in_specs=[pl.BlockSpec((bm, k), lambda i: (i, 0)),
                  pl.BlockSpec((None, k, d), lambda i: (layer, 0, 0)),
                  pl.BlockSpec((bm, d), lambda i: (i, 0)),
                  pl.BlockSpec((1, d), lambda i: (0, 0))],
        out_specs=(pl.BlockSpec((bm, d), lambda i: (i, 0)),
                   pl.BlockSpec((bm, d), lambda i: (i, 0))),
        compiler_params=_params(1),
        name="residual_norm",
    )(a, w, x, g.reshape(1, d))


def _residual_kernel(a_ref, w_ref, x_ref, o_ref):
    o_ref[...] = x_ref[...] + _dot(a_ref[...], w_ref[...])


def _residual(a, w, layer, x, bm, bn):
    t, k = a.shape
    d = w.shape[2]
    return pl.pallas_call(
        _residual_kernel,
        out_shape=jax.ShapeDtypeStruct((t, d), F32),
        grid=(d // bn, t // bm),
        in_specs=[pl.BlockSpec((bm, k), lambda j, i: (i, 0)),
                  pl.BlockSpec((None, k, bn), lambda j, i: (layer, 0, j)),
                  pl.BlockSpec((bm, bn), lambda j, i: (i, j))],
        out_specs=pl.BlockSpec((bm, bn), lambda j, i: (i, j)),
        compiler_params=_params(2),
        name="ffn_down",
    )(a, w, x)


def _ffn_up_kernel(h_ref, wg_ref, wv_ref, cwg_ref, cwv_ref, cbg_ref, cbv_ref, o_ref,
                   sg_ref, sv_ref, wgb_ref, wvb_ref, *, bm, chunk, blocks_per_seq):
    i = pl.program_id(1)

    @pl.when(i == 0)
    def _():
        wgb_ref[...] = wg_ref[...].astype(wgb_ref.dtype)
        wvb_ref[...] = wv_ref[...].astype(wvb_ref.dtype)

    starts_sequence = (i % blocks_per_seq) == 0
    for s_ref in (sg_ref, sv_ref):
        s_ref[0:SUBLANES, :] = jnp.where(starts_sequence, 0.0, s_ref[bm:bm + SUBLANES, :])

    def conv(s_ref, cw_ref, cb_ref, cols):
        lo = SUBLANES
        acc = cb_ref[:, cols] + cw_ref[0:1, cols] * s_ref[lo - 2:lo - 2 + bm, cols]
        acc = acc + cw_ref[1:2, cols] * s_ref[lo - 1:lo - 1 + bm, cols]
        return acc + cw_ref[2:3, cols] * s_ref[lo:lo + bm, cols]

    h = h_ref[...]
    for c in range(o_ref.shape[1] // chunk):
        cols = slice(c * chunk, (c + 1) * chunk)
        sg_ref[SUBLANES:SUBLANES + bm, cols] = _dot(h, wgb_ref[:, cols])
        sv_ref[SUBLANES:SUBLANES + bm, cols] = _dot(h, wvb_ref[:, cols])
        gate = conv(sg_ref, cwg_ref, cbg_ref, cols)
        val = conv(sv_ref, cwv_ref, cbv_ref, cols)
        o_ref[:, cols] = (gate * jax.nn.sigmoid(gate) * val).astype(o_ref.dtype)


def _ffn_up(h, w_up, layer, conv_w, conv_b, seq, bm, bn, chunk):
    t, d = h.shape
    d_ff = w_up.shape[2] // 2
    nj = d_ff // bn
    return pl.pallas_call(
        functools.partial(_ffn_up_kernel, bm=bm, chunk=chunk, blocks_per_seq=seq // bm),
        out_shape=jax.ShapeDtypeStruct((t, d_ff), BF16),
        grid=(nj, t // bm),
        in_specs=[pl.BlockSpec((bm, d), lambda j, i: (i, 0)),
                  pl.BlockSpec((None, d, bn), lambda j, i: (layer, 0, j)),
                  pl.BlockSpec((None, d, bn), lambda j, i: (layer, 0, nj + j)),
                  pl.BlockSpec((CONV_WIDTH, bn), lambda j, i: (0, j)),
                  pl.BlockSpec((CONV_WIDTH, bn), lambda j, i: (0, nj + j)),
                  pl.BlockSpec((1, bn), lambda j, i: (0, j)),
                  pl.BlockSpec((1, bn), lambda j, i: (0, nj + j))],
        out_specs=pl.BlockSpec((bm, bn), lambda j, i: (i, j)),
        scratch_shapes=[pltpu.VMEM((bm + SUBLANES, bn), F32),
                        pltpu.VMEM((bm + SUBLANES, bn), F32),
                        pltpu.VMEM((d, bn), BF16), pltpu.VMEM((d, bn), BF16)],
        compiler_params=_params(2),
        name="ffn_up",
    )(h, w_up, w_up, conv_w, conv_w, conv_b, conv_b)


def _rope_tables(seq):
    half = MLA_ROPE // 2
    inv = 1.0 / (ROPE_THETA ** (jnp.arange(0, MLA_ROPE, 2, dtype=F32) / MLA_ROPE))
    ang = jnp.arange(seq, dtype=F32)[:, None] * inv[None, :]
    cos, sin = jnp.cos(ang), jnp.sin(ang)
    zero = jnp.zeros((seq, half), F32)
    tail = jnp.zeros((seq, LANES - MLA_ROPE), F32)
    c = jnp.concatenate([cos, cos, tail], axis=1)
    s1 = jnp.concatenate([-sin, zero, tail], axis=1)
    s2 = jnp.concatenate([zero, sin, tail], axis=1)
    return c, s1, s2


def _block(n, want):
    b = min(n, want)
    assert n % b == 0, (n, want)
    return b


def _in_layout(d_model):
    src = {"sbq": 0, "sbk": _SBW, "sbv": 2 * _SBW, "cq": 3 * _SBW}
    src["ckv"] = src["cq"] + MLA_Q_LORA
    src["krope"] = src["ckv"] + MLA_KV_LORA
    src["gsb"] = src["krope"] + MLA_ROPE
    src["gmla"] = src["gsb"] + d_model
    src["end"] = src["gmla"] + d_model
    dst = {"sbq": _IN_SBQ, "sbk": _IN_SBK, "sbv": _IN_SBV, "gsb": _IN_GSB,
           "gmla": _IN_GSB + d_model}
    dst["ckv"] = dst["gmla"] + d_model
    dst["cq"] = dst["ckv"] + MLA_KV_LORA
    dst["krope"] = dst["cq"] + MLA_Q_LORA
    dst["end"] = -(-(dst["krope"] + LANES) // _IN_BLOCK) * _IN_BLOCK
    assert dst["ckv"] % MLA_KV_LORA == 0 and dst["cq"] % MLA_Q_LORA == 0
    assert dst["krope"] % LANES == 0 and dst["gmla"] % _IN_BLOCK == 0
    return src, dst


def _prep_w_in_starts(d_model):
    src, dst = _in_layout(d_model)
    widths = {"sbq": _SBW, "sbk": _SBW, "sbv": _SBW, "gsb": d_model, "gmla": d_model,
              "ckv": MLA_KV_LORA, "cq": MLA_Q_LORA}
    starts = []
    for b in range(dst["end"] // _IN_BLOCK):
        r = b * _IN_BLOCK
        name = max((n for n in widths if dst[n] <= r), key=lambda n: dst[n])
        assert r + _IN_BLOCK <= dst[name] + widths[name] or b == dst["end"] // _IN_BLOCK - 1
        starts.append(src[name] + r - dst[name])
    return tuple(starts)


def _prep_w_in_kernel(a_ref, kr_ref, o_ref, *, d_model):
    _, dst = _in_layout(d_model)
    b = pl.program_id(1)
    n_blocks = dst["end"] // _IN_BLOCK
    n_scaled = _SBW // _IN_BLOCK
    cq_rows = dst["krope"] - (n_blocks - 1) * _IN_BLOCK

    @pl.when(b < n_scaled)
    def _():
        o_ref[...] = (a_ref[0] * (SB_HEAD_DIM ** -0.5 * LOG2E)).astype(o_ref.dtype)

    @pl.when((b >= n_scaled) & (b < n_blocks - 1))
    def _():
        o_ref[...] = a_ref[0].astype(o_ref.dtype)

    @pl.when(b == n_blocks - 1)
    def _():
        o_ref[0:cq_rows, :] = a_ref[0, 0:cq_rows, :].astype(o_ref.dtype)
        o_ref[cq_rows:cq_rows + MLA_ROPE, :] = kr_ref[0].astype(o_ref.dtype)
        o_ref[cq_rows + MLA_ROPE:, :] = jnp.zeros(
            (_IN_BLOCK - cq_rows - MLA_ROPE, o_ref.shape[1]), o_ref.dtype)


def _prep_w_in(w_in):
    depth, k, n = w_in.shape
    src, dst = _in_layout(k)
    assert src["end"] == n
    starts = _prep_w_in_starts(k)
    w_t = jnp.swapaxes(w_in, 1, 2)

    def a_map(l, b):
        start = jnp.int32(starts[-1])
        for blk in range(len(starts) - 2, -1, -1):
            start = jnp.where(b == blk, starts[blk], start)
        return (l, pl.multiple_of(start, MLA_ROPE), 0)

    return pl.pallas_call(
        functools.partial(_prep_w_in_kernel, d_model=k),
        out_shape=jax.ShapeDtypeStruct((depth, dst["end"], k), BF16),
        grid=(depth, len(starts)),
        in_specs=[pl.BlockSpec((pl.Element(1), pl.Element(_IN_BLOCK), pl.Element(k)), a_map),
                  pl.BlockSpec((pl.Element(1), pl.Element(MLA_ROPE), pl.Element(k)),
                               lambda l, b: (l, src["krope"], 0))],
        out_specs=pl.BlockSpec((None, _IN_BLOCK, k), lambda l, b: (l, b, 0)),
        compiler_params=_params(2),
        name="prep_w_in",
    )(w_t, w_t)


def _cast_kernel(w_ref, o_ref):
    o_ref[...] = w_ref[...].astype(o_ref.dtype)


def _cast_bf16(w, rows):
    depth, k, n = w.shape
    return pl.pallas_call(
        _cast_kernel,
        out_shape=jax.ShapeDtypeStruct(w.shape, BF16),
        grid=(depth, k // rows),
        in_specs=[pl.BlockSpec((None, rows, n), lambda l, i: (l, i, 0))],
        out_specs=pl.BlockSpec((None, rows, n), lambda l, i: (l, i, 0)),
        compiler_params=_params(2),
        name="cast_bf16",
    )(w)


def _prep_w_uq(w_uq):
    per = MLA_NOPE + MLA_ROPE
    w = w_uq.reshape(w_uq.shape[0], MLA_HEADS, per) * (per ** -0.5 * LOG2E)
    w = jnp.pad(w, ((0, 0), (0, 0), (0, MLA_QK_PAD - per)))
    return w.reshape(w_uq.shape[0], MLA_HEADS * MLA_QK_PAD).astype(BF16)


def _prep_w_ukv(w_ukv):
    w = w_ukv.reshape(w_ukv.shape[0], MLA_HEADS, MLA_NOPE + MLA_V)
    k = w[:, :, :MLA_NOPE].reshape(w_ukv.shape[0], MLA_HEADS * MLA_NOPE)
    v = w[:, :, MLA_NOPE:].reshape(w_ukv.shape[0], MLA_HEADS * MLA_V)
    return jnp.concatenate([k, v], axis=1).astype(BF16)


def kernel(x, norm1_g, w_in, b_gate, q_norm_g, w_uq, kv_norm_g, w_ukv, w_proj_sb, w_proj_mla,
           w_out, norm2_g, w_up, conv_w, conv_b, w_down, final_g):
    batch, seq, d_model = x.shape
    depth = w_in.shape[0]
    t = batch * seq
    tabs = _rope_tables(seq)

    bm_big = _block(seq, 1024)
    bm_mid = _block(seq, 512)
    sb_bq, sb_bk = _block(seq, 4096), _block(seq, 256)
    mla_bq = _block(seq, 4096)
    mla_bk, mla_sub = _block(mla_bq, 512), _block(mla_bq, 256)

    _, cols = _in_layout(d_model)
    w_in_b = _prep_w_in(w_in)
    w_sb_b, w_mla_b = _cast_bf16(w_proj_sb, 512), _cast_bf16(w_proj_mla, 512)
    w_out_b, w_down_b = _cast_bf16(w_out, 512), _cast_bf16(w_down, 512)

    xf = x.reshape(t, d_model)
    for l in range(depth):
        proj = _norm_matmul(xf, norm1_g[l], w_in_b, l, bm_big, _IN_BLOCK)
        q_mla = _mla_q(proj, cols["cq"], q_norm_g[l], _prep_w_uq(w_uq[l]), tabs, bm_mid, seq)
        k_mla, v_mla = _mla_kv(proj, cols["ckv"], cols["krope"], kv_norm_g[l],
                               _prep_w_ukv(w_ukv[l]), tabs, bm_mid, seq)
        o_sb = _sb_attn(proj, batch, seq, sb_bq, sb_bk)
        o_mla = _mla_attn(q_mla, k_mla, v_mla, batch, seq, mla_bq, mla_bk, mla_sub)
        mixed = _merge(o_sb, o_mla, w_sb_b, w_mla_b, l, proj, cols["gsb"], cols["gmla"],
                       b_gate[l].reshape(1, -1), bm_big, 2 * _IN_BLOCK)
        x1, h2 = _residual_norm(mixed, w_out_b, l, xf, norm2_g[l], bm_mid)
        act = _ffn_up(h2, w_up, l, conv_w[l], conv_b[l].reshape(1, -1), seq,
                      bm_big, 512, 256)
        xf = _residual(act, w_down_b, l, x1, bm_mid, 1024)
    return _rmsnorm(xf, final_g, bm_mid).reshape(batch, seq, d_model)
```

```python
import functools
import math

import jax
import jax.numpy as jnp
from jax import lax
from jax.experimental import pallas as pl
from jax.experimental.pallas import tpu as pltpu

F32 = jnp.float32
BF16 = jnp.bfloat16

EPS = 1e-6
LOG2E = math.log2(math.e)
ROPE_THETA = 10000.0
CHUNK = 64
SB_HEADS = 8
SB_HEAD_DIM = 128
MLA_HEADS = 8
MLA_NOPE = 128
MLA_ROPE = 64
MLA_V = 128
MLA_Q_LORA = 768
MLA_KV_LORA = 512
CONV_WIDTH = 3

LANES = 128
SUBLANES = 8
MLA_QK_PAD = 2 * LANES
VMEM_LIMIT_BYTES = 56 * 1024 * 1024

_SBW = SB_HEADS * SB_HEAD_DIM
_IN_SBQ = 0
_IN_SBK = _SBW
_IN_SBV = 2 * _SBW
_IN_GSB = 3 * _SBW
_IN_BLOCK = 512


def _params(n_axes):
    return pltpu.CompilerParams(
        dimension_semantics=("arbitrary",) * n_axes, vmem_limit_bytes=VMEM_LIMIT_BYTES)


def _rms(x, g):
    return x * lax.rsqrt(jnp.mean(x * x, axis=-1, keepdims=True) + EPS) * g


def _rope_group(g, c, s1, s2):
    return g * c + pltpu.roll(g, 96, 1) * s1 + pltpu.roll(g, 32, 1) * s2


def _dot(a, b):
    return jnp.dot(a, b, preferred_element_type=F32)


def _dot_nt(a, b):
    return lax.dot_general(a, b, (((1,), (1,)), ((), ())), preferred_element_type=F32)


def _rmsnorm_kernel(x_ref, g_ref, o_ref):
    o_ref[...] = _rms(x_ref[...], g_ref[...]).astype(o_ref.dtype)


def _rmsnorm(x, g, bm):
    t, d = x.shape
    return pl.pallas_call(
        _rmsnorm_kernel,
        out_shape=jax.ShapeDtypeStruct((t, d), x.dtype),
        grid=(t // bm,),
        in_specs=[pl.BlockSpec((bm, d), lambda i: (i, 0)),
                  pl.BlockSpec((1, d), lambda i: (0, 0))],
        out_specs=pl.BlockSpec((bm, d), lambda i: (i, 0)),
        compiler_params=_params(1),
        name="rmsnorm",
    )(x, g.reshape(1, d))


def _norm_matmul_kernel(x_ref, g_ref, w_ref, o_ref, h_ref):
    @pl.when(pl.program_id(1) == 0)
    def _():
        h_ref[...] = _rms(x_ref[...], g_ref[...]).astype(h_ref.dtype)

    o_ref[...] = _dot_nt(h_ref[...], w_ref[...]).astype(o_ref.dtype)


def _norm_matmul(x, g, w_t, layer, bm, bn):
    t, k = x.shape
    n = w_t.shape[1]
    return pl.pallas_call(
        _norm_matmul_kernel,
        out_shape=jax.ShapeDtypeStruct((t, n), BF16),
        grid=(t // bm, n // bn),
        in_specs=[pl.BlockSpec((bm, k), lambda i, j: (i, 0)),
                  pl.BlockSpec((1, k), lambda i, j: (0, 0)),
                  pl.BlockSpec((None, bn, k), lambda i, j: (layer, j, 0))],
        out_specs=pl.BlockSpec((bm, bn), lambda i, j: (i, j)),
        scratch_shapes=[pltpu.VMEM((bm, k), BF16)],
        compiler_params=_params(2),
        name="in_proj",
    )(x, g.reshape(1, k), w_t)


def _mla_q_kernel(cq_ref, g_ref, w_ref, c_ref, s1_ref, s2_ref, o_ref):
    h = _rms(cq_ref[...].astype(F32), g_ref[...]).astype(BF16)
    y = _dot(h, w_ref[...])
    c, s1, s2 = c_ref[...], s1_ref[...], s2_ref[...]
    for hd in range(MLA_HEADS):
        lo = hd * MLA_QK_PAD
        o_ref[:, lo:lo + LANES] = y[:, lo:lo + LANES].astype(o_ref.dtype)
        rot = _rope_group(y[:, lo + LANES:lo + MLA_QK_PAD], c, s1, s2)
        o_ref[:, lo + LANES:lo + MLA_QK_PAD] = rot.astype(o_ref.dtype)


def _mla_q(proj, cq_col, g, w, tabs, bm, seq):
    t = proj.shape[0]
    n = w.shape[1]
    nsb = seq // bm
    tab_spec = pl.BlockSpec((bm, LANES), lambda i: (i % nsb, 0))
    return pl.pallas_call(
        _mla_q_kernel,
        out_shape=jax.ShapeDtypeStruct((t, n), BF16),
        grid=(t // bm,),
        in_specs=[pl.BlockSpec((bm, MLA_Q_LORA), lambda i: (i, cq_col // MLA_Q_LORA)),
                  pl.BlockSpec((1, MLA_Q_LORA), lambda i: (0, 0)),
                  pl.BlockSpec((MLA_Q_LORA, n), lambda i: (0, 0)),
                  tab_spec, tab_spec, tab_spec],
        out_specs=pl.BlockSpec((bm, n), lambda i: (i, 0)),
        compiler_params=_params(1),
        name="mla_q",
    )(proj, g.reshape(1, MLA_Q_LORA), w, *tabs)


def _mla_kv_kernel(ckv_ref, kr_ref, g_ref, w_ref, c_ref, s1_ref, s2_ref, k_ref, v_ref):
    h = _rms(ckv_ref[...].astype(F32), g_ref[...]).astype(BF16)
    y = _dot(h, w_ref[...])
    rot = _rope_group(kr_ref[...].astype(F32), c_ref[...], s1_ref[...], s2_ref[...])
    rot = rot.astype(k_ref.dtype)
    kw = MLA_HEADS * MLA_NOPE
    for hd in range(MLA_HEADS):
        lo = hd * MLA_QK_PAD
        k_ref[:, lo:lo + LANES] = y[:, hd * MLA_NOPE:(hd + 1) * MLA_NOPE].astype(k_ref.dtype)
        k_ref[:, lo + LANES:lo + MLA_QK_PAD] = rot
    v_ref[...] = y[:, kw:].astype(v_ref.dtype)


def _mla_kv(proj, ckv_col, kr_col, g, w, tabs, bm, seq):
    t = proj.shape[0]
    nsb = seq // bm
    tab_spec = pl.BlockSpec((bm, LANES), lambda i: (i % nsb, 0))
    return pl.pallas_call(
        _mla_kv_kernel,
        out_shape=(jax.ShapeDtypeStruct((t, MLA_HEADS * MLA_QK_PAD), BF16),
                   jax.ShapeDtypeStruct((t, MLA_HEADS * MLA_V), BF16)),
        grid=(t // bm,),
        in_specs=[pl.BlockSpec((bm, MLA_KV_LORA), lambda i: (i, ckv_col // MLA_KV_LORA)),
                  pl.BlockSpec((bm, LANES), lambda i: (i, kr_col // LANES)),
                  pl.BlockSpec((1, MLA_KV_LORA), lambda i: (0, 0)),
                  pl.BlockSpec(w.shape, lambda i: (0, 0)),
                  tab_spec, tab_spec, tab_spec],
        out_specs=(pl.BlockSpec((bm, MLA_HEADS * MLA_QK_PAD), lambda i: (i, 0)),
                   pl.BlockSpec((bm, MLA_HEADS * MLA_V), lambda i: (i, 0))),
        compiler_params=_params(1),
        name="mla_kv",
    )(proj, proj, g.reshape(1, MLA_KV_LORA), w, *tabs)


def _sb_attn_kernel(q_ref, k_ref, v_ref, o_ref, run_ref, acc_ref, *, bq, bk):
    i = pl.program_id(2)
    n_diag = bq // bk
    row = lax.broadcasted_iota(jnp.int32, (bk, bk), 0)
    col = lax.broadcasted_iota(jnp.int32, (bk, bk), 1)
    later = (row > col).astype(BF16)
    later2 = jnp.concatenate([later, later], axis=0)

    run_ref[...] = jnp.zeros(run_ref.shape, F32)
    acc_ref[...] = jnp.zeros(acc_ref.shape, F32)

    def tile(kb, diag):
        rows = slice(0 if diag is None else diag * bk, bq)
        start = pl.multiple_of(kb * bk, bk)
        k = k_ref[pl.ds(start, bk), :]
        v = v_ref[pl.ds(start, bk), :]
        z = _dot_nt(q_ref[rows, :], k)
        z_pos = jnp.maximum(z, 0.0)
        z_neg = z - z_pos
        log_term = jnp.log(1.0 + jnp.exp2(z_neg - z_pos)) * LOG2E
        sp = z_pos + log_term
        if diag is not None:
            causal = (lax.broadcasted_iota(jnp.int32, z.shape, 1)
                      < lax.broadcasted_iota(jnp.int32, z.shape, 0))
            sp = jnp.where(causal, sp, 0.0)
        hi = sp.astype(BF16)
        split = jnp.concatenate([hi, (sp - hi.astype(F32)).astype(BF16)], axis=1)
        suffix = _dot(split, later2)
        a = jnp.exp2(z_neg - log_term - suffix)
        if diag is not None:
            a = jnp.where(causal, a, 0.0)
        run = run_ref[rows, :]
        acc_ref[rows, :] += jnp.exp2(-run) * _dot(a.astype(BF16), v)
        run_ref[rows, :] = run + (suffix[:, 0:1] + sp[:, 0:1])

    for d in range(n_diag - 1, -1, -1):
        tile(i * n_diag + d, d)
    n_before = i * n_diag

    def body(t, carry):
        tile(n_before - 1 - t, None)
        return carry

    lax.fori_loop(0, n_before, body, 0)
    o_ref[...] = acc_ref[...].astype(o_ref.dtype)


def _sb_attn(proj, batch, seq, bq, bk):
    t = proj.shape[0]
    nq = seq // bq
    qc, kc, vc = (c // SB_HEAD_DIM for c in (_IN_SBQ, _IN_SBK, _IN_SBV))
    return pl.pallas_call(
        functools.partial(_sb_attn_kernel, bq=bq, bk=bk),
        out_shape=jax.ShapeDtypeStruct((t, _SBW), BF16),
        grid=(batch, SB_HEADS, nq),
        in_specs=[pl.BlockSpec((bq, SB_HEAD_DIM), lambda b, h, i: (b * nq + i, qc + h)),
                  pl.BlockSpec((seq, SB_HEAD_DIM), lambda b, h, i: (b, kc + h)),
                  pl.BlockSpec((seq, SB_HEAD_DIM), lambda b, h, i: (b, vc + h))],
        out_specs=pl.BlockSpec((bq, SB_HEAD_DIM), lambda b, h, i: (b * nq + i, h)),
        scratch_shapes=[pltpu.VMEM((bq, LANES), F32), pltpu.VMEM((bq, SB_HEAD_DIM), F32)],
        compiler_params=_params(3),
        name="sb_attn",
    )(proj, proj, proj)


def _mla_attn_kernel(q_ref, k_ref, v_ref, o_ref, m_ref, acc_ref, *, bq, bk, sub):
    i = pl.program_id(2)
    n_diag = bq // bk
    m_ref[...] = jnp.full(m_ref.shape, -jnp.inf, F32)
    acc_ref[...] = jnp.zeros(acc_ref.shape, F32)
    ones = jnp.ones((bk, LANES), BF16)

    def tile(kb, diag):
        start = pl.multiple_of(kb * bk, bk)
        k = k_ref[pl.ds(start, bk), :]
        v1 = jnp.concatenate([v_ref[pl.ds(start, bk), :], ones], axis=1)
        for s in range(bq // sub):
            rows = slice(s * sub, (s + 1) * sub)
            if diag is not None and (s + 1) * sub <= diag * bk:
                continue
            sc = _dot_nt(q_ref[rows, :], k)
            if diag is not None and s * sub < (diag + 1) * bk:
                qchunk = (lax.broadcasted_iota(jnp.int32, (sub, bk), 0) + s * sub) // CHUNK
                kchunk = (lax.broadcasted_iota(jnp.int32, (sub, bk), 1) + diag * bk) // CHUNK
                sc = jnp.where(kchunk <= qchunk, sc, -jnp.inf)
            m = m_ref[rows, :]
            m_new = jnp.maximum(m, jnp.max(sc, axis=1, keepdims=True))
            alpha = jnp.exp2(m - m_new)
            p = jnp.exp2(sc - jnp.tile(m_new, (1, bk // LANES)))
            acc_ref[rows, :] = (jnp.tile(alpha, (1, 2)) * acc_ref[rows, :]
                                + _dot(p.astype(BF16), v1))
            m_ref[rows, :] = m_new

    for d in range(n_diag):
        tile(i * n_diag + d, d)
    n_before = i * n_diag

    def body(t, carry):
        tile(n_before - 1 - t, None)
        return carry

    lax.fori_loop(0, n_before, body, 0)
    o_ref[...] = (acc_ref[:, 0:MLA_V] / acc_ref[:, MLA_V:2 * MLA_V]).astype(o_ref.dtype)


def _mla_attn(q, k, v, batch, seq, bq, bk, sub):
    t = q.shape[0]
    nq = seq // bq
    return pl.pallas_call(
        functools.partial(_mla_attn_kernel, bq=bq, bk=bk, sub=sub),
        out_shape=jax.ShapeDtypeStruct((t, MLA_HEADS * MLA_V), BF16),
        grid=(batch, MLA_HEADS, nq),
        in_specs=[pl.BlockSpec((bq, MLA_QK_PAD), lambda b, h, i: (b * nq + i, h)),
                  pl.BlockSpec((seq, MLA_QK_PAD), lambda b, h, i: (b, h)),
                  pl.BlockSpec((seq, MLA_V), lambda b, h, i: (b, h))],
        out_specs=pl.BlockSpec((bq, MLA_V), lambda b, h, i: (b * nq + i, h)),
        scratch_shapes=[pltpu.VMEM((bq, LANES), F32), pltpu.VMEM((bq, 2 * MLA_V), F32)],
        compiler_params=_params(3),
        name="mla_attn",
    )(q, k, v)


def _merge_kernel(osb_ref, omla_ref, wsb_ref, wmla_ref, gsb_ref, gmla_ref, bsb_ref, bmla_ref,
                  o_ref):
    gate_sb = jax.nn.sigmoid(gsb_ref[...].astype(F32) + bsb_ref[...])
    gate_mla = jax.nn.sigmoid(gmla_ref[...].astype(F32) + bmla_ref[...])
    mixed = (gate_sb * _dot(osb_ref[...], wsb_ref[...])
             + gate_mla * _dot(omla_ref[...], wmla_ref[...]))
    o_ref[...] = mixed.astype(o_ref.dtype)


def _merge(o_sb, o_mla, w_sb, w_mla, layer, proj, gsb_col, gmla_col, b_gate, bm, bn):
    t, k = o_sb.shape
    d = w_sb.shape[2]
    nj = d // bn
    return pl.pallas_call(
        _merge_kernel,
        out_shape=jax.ShapeDtypeStruct((t, d), BF16),
        grid=(t // bm, nj),
        in_specs=[pl.BlockSpec((bm, k), lambda i, j: (i, 0)),
                  pl.BlockSpec((bm, k), lambda i, j: (i, 0)),
                  pl.BlockSpec((None, k, bn), lambda i, j: (layer, 0, j)),
                  pl.BlockSpec((None, k, bn), lambda i, j: (layer, 0, j)),
                  pl.BlockSpec((bm, bn), lambda i, j: (i, gsb_col // bn + j)),
                  pl.BlockSpec((bm, bn), lambda i, j: (i, gmla_col // bn + j)),
                  pl.BlockSpec((1, bn), lambda i, j: (0, j)),
                  pl.BlockSpec((1, bn), lambda i, j: (0, nj + j))],
        out_specs=pl.BlockSpec((bm, bn), lambda i, j: (i, j)),
        compiler_params=_params(2),
        name="merge",
    )(o_sb, o_mla, w_sb, w_mla, proj, proj, b_gate, b_gate)


def _residual_norm_kernel(a_ref, w_ref, x_ref, g_ref, o_ref, h_ref):
    out = x_ref[...] + _dot(a_ref[...], w_ref[...])
    o_ref[...] = out
    h_ref[...] = _rms(out, g_ref[...]).astype(h_ref.dtype)


def _residual_norm(a, w, layer, x, g, bm):
    t, k = a.shape
    d = w.shape[2]
    return pl.pallas_call(
        _residual_norm_kernel,
        out_shape=(jax.ShapeDtypeStruct((t, d), F32), jax.ShapeDtypeStruct((t, d), BF16)),
        grid=(t // bm,),
        in_specs=[pl.BlockSpec((bm, k), lambda i: (i, 0)),
                  pl.BlockSpec((None, k, d), lambda i: (layer, 0, 0)),
                  pl.BlockSpec((bm, d), lambda i: (i, 0)),
                  pl.BlockSpec((1, d), lambda i: (0, 0))],
        out_specs=(pl.BlockSpec((bm, d), lambda i: (i, 0)),
                   pl.BlockSpec((bm, d), lambda i: (i, 0))),
        compiler_params=_params(1),
        name="residual_norm",
    )(a, w, x, g.reshape(1, d))


def _residual_kernel(a_ref, w_ref, x_ref, o_ref):
    o_ref[...] = x_ref[...] + _dot(a_ref[...], w_ref[...])


def _residual(a, w, layer, x, bm, bn):
    t, k = a.shape
    d = w.shape[2]
    return pl.pallas_call(
        _residual_kernel,
        out_shape=jax.ShapeDtypeStruct((t, d), F32),
        grid=(d // bn, t // bm),
        in_specs=[pl.BlockSpec((bm, k), lambda j, i: (i, 0)),
                  pl.BlockSpec((None, k, bn), lambda j, i: (layer, 0, j)),
                  pl.BlockSpec((bm, bn), lambda j, i: (i, j))],
        out_specs=pl.BlockSpec((bm, bn), lambda j, i: (i, j)),
        compiler_params=_params(2),
        name="ffn_down",
    )(a, w, x)


def _ffn_up_kernel(h_ref, wg_ref, wv_ref, cwg_ref, cwv_ref, cbg_ref, cbv_ref, o_ref,
                   sg_ref, sv_ref, wgb_ref, wvb_ref, *, bm, chunk, blocks_per_seq):
    i = pl.program_id(1)

    @pl.when(i == 0)
    def _():
        wgb_ref[...] = wg_ref[...].astype(wgb_ref.dtype)
        wvb_ref[...] = wv_ref[...].astype(wvb_ref.dtype)

    starts_sequence = (i % blocks_per_seq) == 0
    for s_ref in (sg_ref, sv_ref):
        s_ref[0:SUBLANES, :] = jnp.where(starts_sequence, 0.0, s_ref[bm:bm + SUBLANES, :])

    def conv(s_ref, cw_ref, cb_ref, cols):
        lo = SUBLANES
        acc = cb_ref[:, cols] + cw_ref[0:1, cols] * s_ref[lo - 2:lo - 2 + bm, cols]
        acc = acc + cw_ref[1:2, cols] * s_ref[lo - 1:lo - 1 + bm, cols]
        return acc + cw_ref[2:3, cols] * s_ref[lo:lo + bm, cols]

    h = h_ref[...]
    chunks = [slice(c * chunk, (c + 1) * chunk) for c in range(o_ref.shape[1] // chunk)]
    for cols in chunks:
        sg_ref[SUBLANES:SUBLANES + bm, cols] = _dot(h, wgb_ref[:, cols])
    sv_ref[SUBLANES:SUBLANES + bm, chunks[0]] = _dot(h, wvb_ref[:, chunks[0]])
    gates = []
    for cols in chunks:
        gate = conv(sg_ref, cwg_ref, cbg_ref, cols)
        gates.append(gate * jax.nn.sigmoid(gate))
    for cols in chunks[1:]:
        sv_ref[SUBLANES:SUBLANES + bm, cols] = _dot(h, wvb_ref[:, cols])
    for cols, gate in zip(chunks, gates):
        val = conv(sv_ref, cwv_ref, cbv_ref, cols)
        o_ref[:, cols] = (gate * val).astype(o_ref.dtype)


def _ffn_up(h, w_up, layer, conv_w, conv_b, seq, bm, bn, chunk):
    t, d = h.shape
    d_ff = w_up.shape[2] // 2
    nj = d_ff // bn
    return pl.pallas_call(
        functools.partial(_ffn_up_kernel, bm=bm, chunk=chunk, blocks_per_seq=seq // bm),
        out_shape=jax.ShapeDtypeStruct((t, d_ff), BF16),
        grid=(nj, t // bm),
        in_specs=[pl.BlockSpec((bm, d), lambda j, i: (i, 0)),
                  pl.BlockSpec((None, d, bn), lambda j, i: (layer, 0, j)),
                  pl.BlockSpec((None, d, bn), lambda j, i: (layer, 0, nj + j)),
                  pl.BlockSpec((CONV_WIDTH, bn), lambda j, i: (0, j)),
                  pl.BlockSpec((CONV_WIDTH, bn), lambda j, i: (0, nj + j)),
                  pl.BlockSpec((1, bn), lambda j, i: (0, j)),
                  pl.BlockSpec((1, bn), lambda j, i: (0, nj + j))],
        out_specs=pl.BlockSpec((bm, bn), lambda j, i: (i, j)),
        scratch_shapes=[pltpu.VMEM((bm + SUBLANES, bn), F32),
                        pltpu.VMEM((bm + SUBLANES, bn), F32),
                        pltpu.VMEM((d, bn), BF16), pltpu.VMEM((d, bn), BF16)],
        compiler_params=_params(2),
        name="ffn_up",
    )(h, w_up, w_up, conv_w, conv_w, conv_b, conv_b)


def _rope_tables(seq):
    half = MLA_ROPE // 2
    inv = 1.0 / (ROPE_THETA ** (jnp.arange(0, MLA_ROPE, 2, dtype=F32) / MLA_ROPE))
    ang = jnp.arange(seq, dtype=F32)[:, None] * inv[None, :]
    cos, sin = jnp.cos(ang), jnp.sin(ang)
    zero = jnp.zeros((seq, half), F32)
    tail = jnp.zeros((seq, LANES - MLA_ROPE), F32)
    c = jnp.concatenate([cos, cos, tail], axis=1)
    s1 = jnp.concatenate([-sin, zero, tail], axis=1)
    s2 = jnp.concatenate([zero, sin, tail], axis=1)
    return c, s1, s2


def _block(n, want):
    b = min(n, want)
    assert n % b == 0, (n, want)
    return b


def _in_layout(d_model):
    src = {"sbq": 0, "sbk": _SBW, "sbv": 2 * _SBW, "cq": 3 * _SBW}
    src["ckv"] = src["cq"] + MLA_Q_LORA
    src["krope"] = src["ckv"] + MLA_KV_LORA
    src["gsb"] = src["krope"] + MLA_ROPE
    src["gmla"] = src["gsb"] + d_model
    src["end"] = src["gmla"] + d_model
    dst = {"sbq": _IN_SBQ, "sbk": _IN_SBK, "sbv": _IN_SBV, "gsb": _IN_GSB,
           "gmla": _IN_GSB + d_model}
    dst["ckv"] = dst["gmla"] + d_model
    dst["cq"] = dst["ckv"] + MLA_KV_LORA
    dst["krope"] = dst["cq"] + MLA_Q_LORA
    dst["end"] = -(-(dst["krope"] + LANES) // _IN_BLOCK) * _IN_BLOCK
    assert dst["ckv"] % MLA_KV_LORA == 0 and dst["cq"] % MLA_Q_LORA == 0
    assert dst["krope"] % LANES == 0 and dst["gmla"] % _IN_BLOCK == 0
    return src, dst


def _prep_w_in_starts(d_model):
    src, dst = _in_layout(d_model)
    widths = {"sbq": _SBW, "sbk": _SBW, "sbv": _SBW, "gsb": d_model, "gmla": d_model,
              "ckv": MLA_KV_LORA, "cq": MLA_Q_LORA}
    starts = []
    for b in range(dst["end"] // _IN_BLOCK):
        r = b * _IN_BLOCK
        name = max((n for n in widths if dst[n] <= r), key=lambda n: dst[n])
        assert r + _IN_BLOCK <= dst[name] + widths[name] or b == dst["end"] // _IN_BLOCK - 1
        starts.append(src[name] + r - dst[name])
    return tuple(starts)


def _prep_w_in_kernel(a_ref, kr_ref, o_ref, *, d_model):
    _, dst = _in_layout(d_model)
    b = pl.program_id(1)
    n_blocks = dst["end"] // _IN_BLOCK
    n_scaled = _SBW // _IN_BLOCK
    cq_rows = dst["krope"] - (n_blocks - 1) * _IN_BLOCK

    @pl.when(b < n_scaled)
    def _():
        o_ref[...] = (a_ref[0] * (SB_HEAD_DIM ** -0.5 * LOG2E)).astype(o_ref.dtype)

    @pl.when((b >= n_scaled) & (b < n_blocks - 1))
    def _():
        o_ref[...] = a_ref[0].astype(o_ref.dtype)

    @pl.when(b == n_blocks - 1)
    def _():
        o_ref[0:cq_rows, :] = a_ref[0, 0:cq_rows, :].astype(o_ref.dtype)
        o_ref[cq_rows:cq_rows + MLA_ROPE, :] = kr_ref[0].astype(o_ref.dtype)
        o_ref[cq_rows + MLA_ROPE:, :] = jnp.zeros(
            (_IN_BLOCK - cq_rows - MLA_ROPE, o_ref.shape[1]), o_ref.dtype)


def _prep_w_in(w_in):
    depth, k, n = w_in.shape
    src, dst = _in_layout(k)
    assert src["end"] == n
    starts = _prep_w_in_starts(k)
    w_t = jnp.swapaxes(w_in, 1, 2)

    def a_map(l, b):
        start = jnp.int32(starts[-1])
        for blk in range(len(starts) - 2, -1, -1):
            start = jnp.where(b == blk, starts[blk], start)
        return (l, pl.multiple_of(start, MLA_ROPE), 0)

    return pl.pallas_call(
        functools.partial(_prep_w_in_kernel, d_model=k),
        out_shape=jax.ShapeDtypeStruct((depth, dst["end"], k), BF16),
        grid=(depth, len(starts)),
        in_specs=[pl.BlockSpec((pl.Element(1), pl.Element(_IN_BLOCK), pl.Element(k)), a_map),
                  pl.BlockSpec((pl.Element(1), pl.Element(MLA_ROPE), pl.Element(k)),
                               lambda l, b: (l, src["krope"], 0))],
        out_specs=pl.BlockSpec((None, _IN_BLOCK, k), lambda l, b: (l, b, 0)),
        compiler_params=_params(2),
        name="prep_w_in",
    )(w_t, w_t)


def _cast_kernel(w_ref, o_ref):
    o_ref[...] = w_ref[...].astype(o_ref.dtype)


def _cast_bf16(w, rows):
    depth, k, n = w.shape
    return pl.pallas_call(
        _cast_kernel,
        out_shape=jax.ShapeDtypeStruct(w.shape, BF16),
        grid=(depth, k // rows),
        in_specs=[pl.BlockSpec((None, rows, n), lambda l, i: (l, i, 0))],
        out_specs=pl.BlockSpec((None, rows, n), lambda l, i: (l, i, 0)),
        compiler_params=_params(2),
        name="cast_bf16",
    )(w)


def _prep_w_uq(w_uq):
    per = MLA_NOPE + MLA_ROPE
    w = w_uq.reshape(w_uq.shape[0], MLA_HEADS, per) * (per ** -0.5 * LOG2E)
    w = jnp.pad(w, ((0, 0), (0, 0), (0, MLA_QK_PAD - per)))
    return w.reshape(w_uq.shape[0], MLA_HEADS * MLA_QK_PAD).astype(BF16)


def _prep_w_ukv(w_ukv):
    w = w_ukv.reshape(w_ukv.shape[0], MLA_HEADS, MLA_NOPE + MLA_V)
    k = w[:, :, :MLA_NOPE].reshape(w_ukv.shape[0], MLA_HEADS * MLA_NOPE)
    v = w[:, :, MLA_NOPE:].reshape(w_ukv.shape[0], MLA_HEADS * MLA_V)
    return jnp.concatenate([k, v], axis=1).astype(BF16)


def kernel(x, norm1_g, w_in, b_gate, q_norm_g, w_uq, kv_norm_g, w_ukv, w_proj_sb, w_proj_mla,
           w_out, norm2_g, w_up, conv_w, conv_b, w_down, final_g):
    batch, seq, d_model = x.shape
    depth = w_in.shape[0]
    t = batch * seq
    tabs = _rope_tables(seq)

    bm_big = _block(seq, 1024)
    bm_mid = _block(seq, 512)
    sb_bq, sb_bk = _block(seq, 4096), _block(seq, 256)
    mla_bq = _block(seq, 4096)
    mla_bk, mla_sub = _block(mla_bq, 512), _block(mla_bq, 256)

    _, cols = _in_layout(d_model)
    w_in_b = _prep_w_in(w_in)
    w_sb_b, w_mla_b = _cast_bf16(w_proj_sb, 512), _cast_bf16(w_proj_mla, 512)
    w_out_b, w_down_b = _cast_bf16(w_out, 512), _cast_bf16(w_down, 512)

    xf = x.reshape(t, d_model)
    for l in range(depth):
        proj = _norm_matmul(xf, norm1_g[l], w_in_b, l, bm_big, _IN_BLOCK)
        q_mla = _mla_q(proj, cols["cq"], q_norm_g[l], _prep_w_uq(w_uq[l]), tabs, bm_mid, seq)
        k_mla, v_mla = _mla_kv(proj, cols["ckv"], cols["krope"], kv_norm_g[l],
                               _prep_w_ukv(w_ukv[l]), tabs, bm_mid, seq)
        o_sb = _sb_attn(proj, batch, seq, sb_bq, sb_bk)
        o_mla = _mla_attn(q_mla, k_mla, v_mla, batch, seq, mla_bq, mla_bk, mla_sub)
        mixed = _merge(o_sb, o_mla, w_sb_b, w_mla_b, l, proj, cols["gsb"], cols["gmla"],
                       b_gate[l].reshape(1, -1), bm_big, 2 * _IN_BLOCK)
        x1, h2 = _residual_norm(mixed, w_out_b, l, xf, norm2_g[l], bm_mid)
        act = _ffn_up(h2, w_up, l, conv_w[l], conv_b[l].reshape(1, -1), seq,
                      bm_big, 512, 256)
        xf = _residual(act, w_down_b, l, x1, bm_mid, 1024)
    return _rmsnorm(xf, final_g, bm_mid).reshape(batch, seq, d_model)
```

```python
import functools
import math

import jax
import jax.numpy as jnp
from jax import lax
from jax.experimental import pallas as pl
from jax.experimental.pallas import tpu as pltpu

F32 = jnp.float32
BF16 = jnp.bfloat16

EPS = 1e-6
LOG2E = math.log2(math.e)
ROPE_THETA = 10000.0
CHUNK = 64
SB_HEADS = 8
SB_HEAD_DIM = 128
MLA_HEADS = 8
MLA_NOPE = 128
MLA_ROPE = 64
MLA_V = 128
MLA_Q_LORA = 768
MLA_KV_LORA = 512
CONV_WIDTH = 3

LANES = 128
SUBLANES = 8
MLA_QK_PAD = 2 * LANES
VMEM_LIMIT_BYTES = 56 * 1024 * 1024

_SBW = SB_HEADS * SB_HEAD_DIM
_IN_SBQ = 0
_IN_SBK = _SBW
_IN_SBV = 2 * _SBW
_IN_GSB = 3 * _SBW
_IN_BLOCK = 512


def _params(n_axes):
    return pltpu.CompilerParams(
        dimension_semantics=("arbitrary",) * n_axes, vmem_limit_bytes=VMEM_LIMIT_BYTES)


def _rms(x, g):
    return x * lax.rsqrt(jnp.mean(x * x, axis=-1, keepdims=True) + EPS) * g


def _rope_group(g, c, s1, s2):
    return g * c + pltpu.roll(g, 96, 1) * s1 + pltpu.roll(g, 32, 1) * s2


def _dot(a, b):
    return jnp.dot(a, b, preferred_element_type=F32)


def _dot_nt(a, b):
    return lax.dot_general(a, b, (((1,), (1,)), ((), ())), preferred_element_type=F32)


def _rmsnorm_kernel(x_ref, g_ref, o_ref):
    o_ref[...] = _rms(x_ref[...], g_ref[...]).astype(o_ref.dtype)


def _rmsnorm(x, g, bm):
    t, d = x.shape
    return pl.pallas_call(
        _rmsnorm_kernel,
        out_shape=jax.ShapeDtypeStruct((t, d), x.dtype),
        grid=(t // bm,),
        in_specs=[pl.BlockSpec((bm, d), lambda i: (i, 0)),
                  pl.BlockSpec((1, d), lambda i: (0, 0))],
        out_specs=pl.BlockSpec((bm, d), lambda i: (i, 0)),
        compiler_params=_params(1),
        name="rmsnorm",
    )(x, g.reshape(1, d))


def _norm_matmul_kernel(x_ref, g_ref, w_ref, o_ref, h_ref):
    @pl.when(pl.program_id(1) == 0)
    def _():
        h_ref[...] = _rms(x_ref[...], g_ref[...]).astype(h_ref.dtype)

    o_ref[...] = _dot_nt(h_ref[...], w_ref[...]).astype(o_ref.dtype)


def _norm_matmul(x, g, w_t, layer, bm, bn):
    t, k = x.shape
    n = w_t.shape[1]
    return pl.pallas_call(
        _norm_matmul_kernel,
        out_shape=jax.ShapeDtypeStruct((t, n), BF16),
        grid=(t // bm, n // bn),
        in_specs=[pl.BlockSpec((bm, k), lambda i, j: (i, 0)),
                  pl.BlockSpec((1, k), lambda i, j: (0, 0)),
                  pl.BlockSpec((None, bn, k), lambda i, j: (layer, j, 0))],
        out_specs=pl.BlockSpec((bm, bn), lambda i, j: (i, j)),
        scratch_shapes=[pltpu.VMEM((bm, k), BF16)],
        compiler_params=_params(2),
        name="in_proj",
    )(x, g.reshape(1, k), w_t)


def _mla_q_kernel(cq_ref, g_ref, w_ref, c_ref, s1_ref, s2_ref, o_ref):
    h = _rms(cq_ref[...].astype(F32), g_ref[...]).astype(BF16)
    y = _dot(h, w_ref[...])
    c, s1, s2 = c_ref[...], s1_ref[...], s2_ref[...]
    for hd in range(MLA_HEADS):
        lo = hd * MLA_QK_PAD
        o_ref[:, lo:lo + LANES] = y[:, lo:lo + LANES].astype(o_ref.dtype)
        rot = _rope_group(y[:, lo + LANES:lo + MLA_QK_PAD], c, s1, s2)
        o_ref[:, lo + LANES:lo + MLA_QK_PAD] = rot.astype(o_ref.dtype)


def _mla_q(proj, cq_col, g, w, tabs, bm, seq):
    t = proj.shape[0]
    n = w.shape[1]
    nsb = seq // bm
    tab_spec = pl.BlockSpec((bm, LANES), lambda i: (i % nsb, 0))
    return pl.pallas_call(
        _mla_q_kernel,
        out_shape=jax.ShapeDtypeStruct((t, n), BF16),
        grid=(t // bm,),
        in_specs=[pl.BlockSpec((bm, MLA_Q_LORA), lambda i: (i, cq_col // MLA_Q_LORA)),
                  pl.BlockSpec((1, MLA_Q_LORA), lambda i: (0, 0)),
                  pl.BlockSpec((MLA_Q_LORA, n), lambda i: (0, 0)),
                  tab_spec, tab_spec, tab_spec],
        out_specs=pl.BlockSpec((bm, n), lambda i: (i, 0)),
        compiler_params=_params(1),
        name="mla_q",
    )(proj, g.reshape(1, MLA_Q_LORA), w, *tabs)


def _mla_kv_kernel(ckv_ref, kr_ref, g_ref, w_ref, c_ref, s1_ref, s2_ref, k_ref, v_ref):
    h = _rms(ckv_ref[...].astype(F32), g_ref[...]).astype(BF16)
    y = _dot(h, w_ref[...])
    rot = _rope_group(kr_ref[...].astype(F32), c_ref[...], s1_ref[...], s2_ref[...])
    rot = rot.astype(k_ref.dtype)
    kw = MLA_HEADS * MLA_NOPE
    for hd in range(MLA_HEADS):
        lo = hd * MLA_QK_PAD
        k_ref[:, lo:lo + LANES] = y[:, hd * MLA_NOPE:(hd + 1) * MLA_NOPE].astype(k_ref.dtype)
        k_ref[:, lo + LANES:lo + MLA_QK_PAD] = rot
    v_ref[...] = y[:, kw:].astype(v_ref.dtype)


def _mla_kv(proj, ckv_col, kr_col, g, w, tabs, bm, seq):
    t = proj.shape[0]
    nsb = seq // bm
    tab_spec = pl.BlockSpec((bm, LANES), lambda i: (i % nsb, 0))
    return pl.pallas_call(
        _mla_kv_kernel,
        out_shape=(jax.ShapeDtypeStruct((t, MLA_HEADS * MLA_QK_PAD), BF16),
                   jax.ShapeDtypeStruct((t, MLA_HEADS * MLA_V), BF16)),
        grid=(t // bm,),
        in_specs=[pl.BlockSpec((bm, MLA_KV_LORA), lambda i: (i, ckv_col // MLA_KV_LORA)),
                  pl.BlockSpec((bm, LANES), lambda i: (i, kr_col // LANES)),
                  pl.BlockSpec((1, MLA_KV_LORA), lambda i: (0, 0)),
                  pl.BlockSpec(w.shape, lambda i: (0, 0)),
                  tab_spec, tab_spec, tab_spec],
        out_specs=(pl.BlockSpec((bm, MLA_HEADS * MLA_QK_PAD), lambda i: (i, 0)),
                   pl.BlockSpec((bm, MLA_HEADS * MLA_V), lambda i: (i, 0))),
        compiler_params=_params(1),
        name="mla_kv",
    )(proj, proj, g.reshape(1, MLA_KV_LORA), w, *tabs)


def _sb_attn_kernel(q_ref, k_ref, v_ref, o_ref, run_ref, acc_ref, *, bq, bk):
    i = pl.program_id(2)
    n_diag = bq // bk
    row = lax.broadcasted_iota(jnp.int32, (bk, bk), 0)
    col = lax.broadcasted_iota(jnp.int32, (bk, bk), 1)
    later = (row > col).astype(BF16)
    later2 = jnp.concatenate([later, later], axis=0)

    run_ref[...] = jnp.zeros(run_ref.shape, F32)
    acc_ref[...] = jnp.zeros(acc_ref.shape, F32)

    def mask_top(x):
        top = jnp.where(row > col, x[:bk], 0.0)
        return top if x.shape[0] == bk else jnp.concatenate([top, x[bk:]], axis=0)

    def tile(kb, diag):
        rows = slice(0 if diag is None else diag * bk, bq)
        start = pl.multiple_of(kb * bk, bk)
        k = k_ref[pl.ds(start, bk), :]
        v = v_ref[pl.ds(start, bk), :]
        z = _dot_nt(q_ref[rows, :], k)
        z_pos = jnp.maximum(z, 0.0)
        z_neg = z - z_pos
        log_term = jnp.log(1.0 + jnp.exp2(z_neg - z_pos)) * LOG2E
        sp = z_pos + log_term
        if diag is not None:
            sp = mask_top(sp)
        hi = sp.astype(BF16)
        split = jnp.concatenate([hi, (sp - hi.astype(F32)).astype(BF16)], axis=1)
        suffix = _dot(split, later2)
        a = jnp.exp2(z_neg - log_term - suffix)
        if diag is not None:
            a = mask_top(a)
        run = run_ref[rows, :]
        acc_ref[rows, :] += jnp.exp2(-run) * _dot(a.astype(BF16), v)
        run_ref[rows, :] = run + (suffix[:, 0:1] + sp[:, 0:1])

    for d in range(n_diag - 1, -1, -1):
        tile(i * n_diag + d, d)
    n_before = i * n_diag

    def body(t, carry):
        tile(n_before - 1 - t, None)
        return carry

    lax.fori_loop(0, n_before, body, 0)
    o_ref[...] = acc_ref[...].astype(o_ref.dtype)


def _sb_attn(proj, batch, seq, bq, bk):
    t = proj.shape[0]
    nq = seq // bq
    qc, kc, vc = (c // SB_HEAD_DIM for c in (_IN_SBQ, _IN_SBK, _IN_SBV))
    return pl.pallas_call(
        functools.partial(_sb_attn_kernel, bq=bq, bk=bk),
        out_shape=jax.ShapeDtypeStruct((t, _SBW), BF16),
        grid=(batch, SB_HEADS, nq),
        in_specs=[pl.BlockSpec((bq, SB_HEAD_DIM), lambda b, h, i: (b * nq + i, qc + h)),
                  pl.BlockSpec((seq, SB_HEAD_DIM), lambda b, h, i: (b, kc + h)),
                  pl.BlockSpec((seq, SB_HEAD_DIM), lambda b, h, i: (b, vc + h))],
        out_specs=pl.BlockSpec((bq, SB_HEAD_DIM), lambda b, h, i: (b * nq + i, h)),
        scratch_shapes=[pltpu.VMEM((bq, LANES), F32), pltpu.VMEM((bq, SB_HEAD_DIM), F32)],
        compiler_params=_params(3),
        name="sb_attn",
    )(proj, proj, proj)


def _mla_attn_kernel(q_ref, k_ref, v_ref, o_ref, m_ref, acc_ref, *, bq, bk, sub):
    i = pl.program_id(2)
    n_diag = bq // bk
    m_ref[...] = jnp.full(m_ref.shape, -jnp.inf, F32)
    acc_ref[...] = jnp.zeros(acc_ref.shape, F32)
    ones = jnp.ones((bk, LANES), BF16)

    def tile(kb, diag):
        start = pl.multiple_of(kb * bk, bk)
        k = k_ref[pl.ds(start, bk), :]
        v1 = jnp.concatenate([v_ref[pl.ds(start, bk), :], ones], axis=1)
        for s in range(bq // sub):
            rows = slice(s * sub, (s + 1) * sub)
            if diag is not None and (s + 1) * sub <= diag * bk:
                continue
            sc = _dot_nt(q_ref[rows, :], k)
            if diag is not None and s * sub < (diag + 1) * bk:
                qchunk = (lax.broadcasted_iota(jnp.int32, (sub, bk), 0) + s * sub) // CHUNK
                kchunk = (lax.broadcasted_iota(jnp.int32, (sub, bk), 1) + diag * bk) // CHUNK
                sc = jnp.where(kchunk <= qchunk, sc, -jnp.inf)
            m = m_ref[rows, :]
            m_new = jnp.maximum(m, jnp.max(sc, axis=1, keepdims=True))
            alpha = jnp.exp2(m - m_new)
            p = jnp.exp2(sc - jnp.tile(m_new, (1, bk // LANES)))
            acc_ref[rows, :] = (jnp.tile(alpha, (1, 2)) * acc_ref[rows, :]
                                + _dot(p.astype(BF16), v1))
            m_ref[rows, :] = m_new

    for d in range(n_diag):
        tile(i * n_diag + d, d)
    n_before = i * n_diag

    def body(t, carry):
        tile(n_before - 1 - t, None)
        return carry

    lax.fori_loop(0, n_before, body, 0)
    o_ref[...] = (acc_ref[:, 0:MLA_V] / acc_ref[:, MLA_V:2 * MLA_V]).astype(o_ref.dtype)


def _mla_attn(q, k, v, batch, seq, bq, bk, sub):
    t = q.shape[0]
    nq = seq // bq
    return pl.pallas_call(
        functools.partial(_mla_attn_kernel, bq=bq, bk=bk, sub=sub),
        out_shape=jax.ShapeDtypeStruct((t, MLA_HEADS * MLA_V), BF16),
        grid=(batch, MLA_HEADS, nq),
        in_specs=[pl.BlockSpec((bq, MLA_QK_PAD), lambda b, h, i: (b * nq + i, h)),
                  pl.BlockSpec((seq, MLA_QK_PAD), lambda b, h, i: (b, h)),
                  pl.BlockSpec((seq, MLA_V), lambda b, h, i: (b, h))],
        out_specs=pl.BlockSpec((bq, MLA_V), lambda b, h, i: (b * nq + i, h)),
        scratch_shapes=[pltpu.VMEM((bq, LANES), F32), pltpu.VMEM((bq, 2 * MLA_V), F32)],
        compiler_params=_params(3),
        name="mla_attn",
    )(q, k, v)


def _merge_kernel(osb_ref, omla_ref, wsb_ref, wmla_ref, gsb_ref, gmla_ref, bsb_ref, bmla_ref,
                  o_ref):
    gate_sb = jax.nn.sigmoid(gsb_ref[...].astype(F32) + bsb_ref[...])
    gate_mla = jax.nn.sigmoid(gmla_ref[...].astype(F32) + bmla_ref[...])
    mixed = (gate_sb * _dot(osb_ref[...], wsb_ref[...])
             + gate_mla * _dot(omla_ref[...], wmla_ref[...]))
    o_ref[...] = mixed.astype(o_ref.dtype)


def _merge(o_sb, o_mla, w_sb, w_mla, layer, proj, gsb_col, gmla_col, b_gate, bm, bn):
    t, k = o_sb.shape
    d = w_sb.shape[2]
    nj = d // bn
    return pl.pallas_call(
        _merge_kernel,
        out_shape=jax.ShapeDtypeStruct((t, d), BF16),
        grid=(t // bm, nj),
        in_specs=[pl.BlockSpec((bm, k), lambda i, j: (i, 0)),
                  pl.BlockSpec((bm, k), lambda i, j: (i, 0)),
                  pl.BlockSpec((None, k, bn), lambda i, j: (layer, 0, j)),
                  pl.BlockSpec((None, k, bn), lambda i, j: (layer, 0, j)),
                  pl.BlockSpec((bm, bn), lambda i, j: (i, gsb_col // bn + j)),
                  pl.BlockSpec((bm, bn), lambda i, j: (i, gmla_col // bn + j)),
                  pl.BlockSpec((1, bn), lambda i, j: (0, j)),
                  pl.BlockSpec((1, bn), lambda i, j: (0, nj + j))],
        out_specs=pl.BlockSpec((bm, bn), lambda i, j: (i, j)),
        compiler_params=_params(2),
        name="merge",
    )(o_sb, o_mla, w_sb, w_mla, proj, proj, b_gate, b_gate)


def _residual_norm_kernel(a_ref, w_ref, x_ref, g_ref, o_ref, h_ref):
    out = x_ref[...] + _dot(a_ref[...], w_ref[...])
    o_ref[...] = out
    h_ref[...] = _rms(out, g_ref[...]).astype(h_ref.dtype)


def _residual_norm(a, w, layer, x, g, bm):
    t, k = a.shape
    d = w.shape[2]
    return pl.pallas_call(
        _residual_norm_kernel,
        out_shape=(jax.ShapeDtypeStruct((t, d), F32), jax.ShapeDtypeStruct((t, d), BF16)),
        grid=(t // bm,),
        in_specs=[pl.BlockSpec((bm, k), lambda i: (i, 0)),
                  pl.BlockSpec((None, k, d), lambda i: (layer, 0, 0)),
                  pl.BlockSpec((bm, d), lambda i: (i, 0)),
                  pl.BlockSpec((1, d), lambda i: (0, 0))],
        out_specs=(pl.BlockSpec((bm, d), lambda i: (i, 0)),
                   pl.BlockSpec((bm, d), lambda i: (i, 0))),
        compiler_params=_params(1),
        name="residual_norm",
    )(a, w, x, g.reshape(1, d))


def _residual_kernel(a_ref, w_ref, x_ref, o_ref):
    o_ref[...] = x_ref[...] + _dot(a_ref[...], w_ref[...])


def _residual(a, w, layer, x, bm, bn):
    t, k = a.shape
    d = w.shape[2]
    return pl.pallas_call(
        _residual_kernel,
        out_shape=jax.ShapeDtypeStruct((t, d), F32),
        grid=(d // bn, t // bm),
        in_specs=[pl.BlockSpec((bm, k), lambda j, i: (i, 0)),
                  pl.BlockSpec((None, k, bn), lambda j, i: (layer, 0, j)),
                  pl.BlockSpec((bm, bn), lambda j, i: (i, j))],
        out_specs=pl.BlockSpec((bm, bn), lambda j, i: (i, j)),
        compiler_params=_params(2),
        name="ffn_down",
    )(a, w, x)


def _ffn_up_kernel(h_ref, wg_ref, wv_ref, cwg_ref, cwv_ref, cbg_ref, cbv_ref, o_ref,
                   sg_ref, sv_ref, wgb_ref, wvb_ref, *, bm, chunk, blocks_per_seq):
    i = pl.program_id(1)

    @pl.when(i == 0)
    def _():
        wgb_ref[...] = wg_ref[...].astype(wgb_ref.dtype)
        wvb_ref[...] = wv_ref[...].astype(wvb_ref.dtype)

    starts_sequence = (i % blocks_per_seq) == 0
    for s_ref in (sg_ref, sv_ref):
        s_ref[0:SUBLANES, :] = jnp.where(starts_sequence, 0.0, s_ref[bm:bm + SUBLANES, :])

    def conv(s_ref, cw_ref, cb_ref, cols):
        lo = SUBLANES
        acc = cb_ref[:, cols] + cw_ref[0:1, cols] * s_ref[lo - 2:lo - 2 + bm, cols]
        acc = acc + cw_ref[1:2, cols] * s_ref[lo - 1:lo - 1 + bm, cols]
        return acc + cw_ref[2:3, cols] * s_ref[lo:lo + bm, cols]

    h = h_ref[...]
    chunks = [slice(c * chunk, (c + 1) * chunk) for c in range(o_ref.shape[1] // chunk)]
    for cols in chunks:
        sg_ref[SUBLANES:SUBLANES + bm, cols] = _dot(h, wgb_ref[:, cols])
    sv_ref[SUBLANES:SUBLANES + bm, chunks[0]] = _dot(h, wvb_ref[:, chunks[0]])
    gates = []
    for cols in chunks:
        gate = conv(sg_ref, cwg_ref, cbg_ref, cols)
        gates.append(gate * jax.nn.sigmoid(gate))
    for cols in chunks[1:]:
        sv_ref[SUBLANES:SUBLANES + bm, cols] = _dot(h, wvb_ref[:, cols])
    for cols, gate in zip(chunks, gates):
        val = conv(sv_ref, cwv_ref, cbv_ref, cols)
        o_ref[:, cols] = (gate * val).astype(o_ref.dtype)


def _ffn_up(h, w_up, layer, conv_w, conv_b, seq, bm, bn, chunk):
    t, d = h.shape
    d_ff = w_up.shape[2] // 2
    nj = d_ff // bn
    return pl.pallas_call(
        functools.partial(_ffn_up_kernel, bm=bm, chunk=chunk, blocks_per_seq=seq // bm),
        out_shape=jax.ShapeDtypeStruct((t, d_ff), BF16),
        grid=(nj, t // bm),
        in_specs=[pl.BlockSpec((bm, d), lambda j, i: (i, 0)),
                  pl.BlockSpec((None, d, bn), lambda j, i: (layer, 0, j)),
                  pl.BlockSpec((None, d, bn), lambda j, i: (layer, 0, nj + j)),
                  pl.BlockSpec((CONV_WIDTH, bn), lambda j, i: (0, j)),
                  pl.BlockSpec((CONV_WIDTH, bn), lambda j, i: (0, nj + j)),
                  pl.BlockSpec((1, bn), lambda j, i: (0, j)),
                  pl.BlockSpec((1, bn), lambda j, i: (0, nj + j))],
        out_specs=pl.BlockSpec((bm, bn), lambda j, i: (i, j)),
        scratch_shapes=[pltpu.VMEM((bm + SUBLANES, bn), F32),
                        pltpu.VMEM((bm + SUBLANES, bn), F32),
                        pltpu.VMEM((d, bn), BF16), pltpu.VMEM((d, bn), BF16)],
        compiler_params=_params(2),
        name="ffn_up",
    )(h, w_up, w_up, conv_w, conv_w, conv_b, conv_b)


def _rope_tables(seq):
    half = MLA_ROPE // 2
    inv = 1.0 / (ROPE_THETA ** (jnp.arange(0, MLA_ROPE, 2, dtype=F32) / MLA_ROPE))
    ang = jnp.arange(seq, dtype=F32)[:, None] * inv[None, :]
    cos, sin = jnp.cos(ang), jnp.sin(ang)
    zero = jnp.zeros((seq, half), F32)
    tail = jnp.zeros((seq, LANES - MLA_ROPE), F32)
    c = jnp.concatenate([cos, cos, tail], axis=1)
    s1 = jnp.concatenate([-sin, zero, tail], axis=1)
    s2 = jnp.concatenate([zero, sin, tail], axis=1)
    return c, s1, s2


def _block(n, want):
    b = min(n, want)
    assert n % b == 0, (n, want)
    return b


def _in_layout(d_model):
    src = {"sbq": 0, "sbk": _SBW, "sbv": 2 * _SBW, "cq": 3 * _SBW}
    src["ckv"] = src["cq"] + MLA_Q_LORA
    src["krope"] = src["ckv"] + MLA_KV_LORA
    src["gsb"] = src["krope"] + MLA_ROPE
    src["gmla"] = src["gsb"] + d_model
    src["end"] = src["gmla"] + d_model
    dst = {"sbq": _IN_SBQ, "sbk": _IN_SBK, "sbv": _IN_SBV, "gsb": _IN_GSB,
           "gmla": _IN_GSB + d_model}
    dst["ckv"] = dst["gmla"] + d_model
    dst["cq"] = dst["ckv"] + MLA_KV_LORA
    dst["krope"] = dst["cq"] + MLA_Q_LORA
    dst["end"] = -(-(dst["krope"] + LANES) // _IN_BLOCK) * _IN_BLOCK
    assert dst["ckv"] % MLA_KV_LORA == 0 and dst["cq"] % MLA_Q_LORA == 0
    assert dst["krope"] % LANES == 0 and dst["gmla"] % _IN_BLOCK == 0
    return src, dst


def _prep_w_in_starts(d_model):
    src, dst = _in_layout(d_model)
    widths = {"sbq": _SBW, "sbk": _SBW, "sbv": _SBW, "gsb": d_model, "gmla": d_model,
              "ckv": MLA_KV_LORA, "cq": MLA_Q_LORA}
    starts = []
    for b in range(dst["end"] // _IN_BLOCK):
        r = b * _IN_BLOCK
        name = max((n for n in widths if dst[n] <= r), key=lambda n: dst[n])
        assert r + _IN_BLOCK <= dst[name] + widths[name] or b == dst["end"] // _IN_BLOCK - 1
        starts.append(src[name] + r - dst[name])
    return tuple(starts)


def _prep_w_in_kernel(a_ref, kr_ref, o_ref, *, d_model):
    _, dst = _in_layout(d_model)
    b = pl.program_id(1)
    n_blocks = dst["end"] // _IN_BLOCK
    n_scaled = _SBW // _IN_BLOCK
    cq_rows = dst["krope"] - (n_blocks - 1) * _IN_BLOCK

    @pl.when(b < n_scaled)
    def _():
        o_ref[...] = (a_ref[0] * (SB_HEAD_DIM ** -0.5 * LOG2E)).astype(o_ref.dtype)

    @pl.when((b >= n_scaled) & (b < n_blocks - 1))
    def _():
        o_ref[...] = a_ref[0].astype(o_ref.dtype)

    @pl.when(b == n_blocks - 1)
    def _():
        o_ref[0:cq_rows, :] = a_ref[0, 0:cq_rows, :].astype(o_ref.dtype)
        o_ref[cq_rows:cq_rows + MLA_ROPE, :] = kr_ref[0].astype(o_ref.dtype)
        o_ref[cq_rows + MLA_ROPE:, :] = jnp.zeros(
            (_IN_BLOCK - cq_rows - MLA_ROPE, o_ref.shape[1]), o_ref.dtype)


def _prep_w_in(w_in):
    depth, k, n = w_in.shape
    src, dst = _in_layout(k)
    assert src["end"] == n
    starts = _prep_w_in_starts(k)
    w_t = jnp.swapaxes(w_in, 1, 2)

    def a_map(l, b):
        start = jnp.int32(starts[-1])
        for blk in range(len(starts) - 2, -1, -1):
            start = jnp.where(b == blk, starts[blk], start)
        return (l, pl.multiple_of(start, MLA_ROPE), 0)

    return pl.pallas_call(
        functools.partial(_prep_w_in_kernel, d_model=k),
        out_shape=jax.ShapeDtypeStruct((depth, dst["end"], k), BF16),
        grid=(depth, len(starts)),
        in_specs=[pl.BlockSpec((pl.Element(1), pl.Element(_IN_BLOCK), pl.Element(k)), a_map),
                  pl.BlockSpec((pl.Element(1), pl.Element(MLA_ROPE), pl.Element(k)),
                               lambda l, b: (l, src["krope"], 0))],
        out_specs=pl.BlockSpec((None, _IN_BLOCK, k), lambda l, b: (l, b, 0)),
        compiler_params=_params(2),
        name="prep_w_in",
    )(w_t, w_t)


def _cast_kernel(w_ref, o_ref):
    o_ref[...] = w_ref[...].astype(o_ref.dtype)


def _cast_bf16(w, rows):
    depth, k, n = w.shape
    return pl.pallas_call(
        _cast_kernel,
        out_shape=jax.ShapeDtypeStruct(w.shape, BF16),
        grid=(depth, k // rows),
        in_specs=[pl.BlockSpec((None, rows, n), lambda l, i: (l, i, 0))],
        out_specs=pl.BlockSpec((None, rows, n), lambda l, i: (l, i, 0)),
        compiler_params=_params(2),
        name="cast_bf16",
    )(w)


def _prep_w_uq(w_uq):
    per = MLA_NOPE + MLA_ROPE
    w = w_uq.reshape(w_uq.shape[0], MLA_HEADS, per) * (per ** -0.5 * LOG2E)
    w = jnp.pad(w, ((0, 0), (0, 0), (0, MLA_QK_PAD - per)))
    return w.reshape(w_uq.shape[0], MLA_HEADS * MLA_QK_PAD).astype(BF16)


def _prep_w_ukv(w_ukv):
    w = w_ukv.reshape(w_ukv.shape[0], MLA_HEADS, MLA_NOPE + MLA_V)
    k = w[:, :, :MLA_NOPE].reshape(w_ukv.shape[0], MLA_HEADS * MLA_NOPE)
    v = w[:, :, MLA_NOPE:].reshape(w_ukv.shape[0], MLA_HEADS * MLA_V)
    return jnp.concatenate([k, v], axis=1).astype(BF16)


def kernel(x, norm1_g, w_in, b_gate, q_norm_g, w_uq, kv_norm_g, w_ukv, w_proj_sb, w_proj_mla,
           w_out, norm2_g, w_up, conv_w, conv_b, w_down, final_g):
    batch, seq, d_model = x.shape
    depth = w_in.shape[0]
    t = batch * seq
    tabs = _rope_tables(seq)

    bm_big = _block(seq, 1024)
    bm_mid = _block(seq, 512)
    sb_bq, sb_bk = _block(seq, 4096), _block(seq, 256)
    mla_bq = _block(seq, 4096)
    mla_bk, mla_sub = _block(mla_bq, 512), _block(mla_bq, 256)

    _, cols = _in_layout(d_model)
    w_in_b = _prep_w_in(w_in)
    w_sb_b, w_mla_b = _cast_bf16(w_proj_sb, 512), _cast_bf16(w_proj_mla, 512)
    w_out_b, w_down_b = _cast_bf16(w_out, 512), _cast_bf16(w_down, 512)

    xf = x.reshape(t, d_model)
    for l in range(depth):
        proj = _norm_matmul(xf, norm1_g[l], w_in_b, l, bm_big, _IN_BLOCK)
        q_mla = _mla_q(proj, cols["cq"], q_norm_g[l], _prep_w_uq(w_uq[l]), tabs, bm_mid, seq)
        k_mla, v_mla = _mla_kv(proj, cols["ckv"], cols["krope"], kv_norm_g[l],
                               _prep_w_ukv(w_ukv[l]), tabs, bm_mid, seq)
        o_sb = _sb_attn(proj, batch, seq, sb_bq, sb_bk)
        o_mla = _mla_attn(q_mla, k_mla, v_mla, batch, seq, mla_bq, mla_bk, mla_sub)
        mixed = _merge(o_sb, o_mla, w_sb_b, w_mla_b, l, proj, cols["gsb"], cols["gmla"],
                       b_gate[l].reshape(1, -1), bm_big, 2 * _IN_BLOCK)
        x1, h2 = _residual_norm(mixed, w_out_b, l, xf, norm2_g[l], bm_mid)
        act = _ffn_up(h2, w_up, l, conv_w[l], conv_b[l].reshape(1, -1), seq,
                      bm_big, 512, 256)
        xf = _residual(act, w_down_b, l, x1, bm_mid, 1024)
    return _rmsnorm(xf, final_g, bm_mid).reshape(batch, seq, d_model)
```

```python
import functools
import math

import jax
import jax.numpy as jnp
from jax import lax
from jax.experimental import pallas as pl
from jax.experimental.pallas import tpu as pltpu

F32 = jnp.float32
BF16 = jnp.bfloat16

EPS = 1e-6
LOG2E = math.log2(math.e)
ROPE_THETA = 10000.0
CHUNK = 64
SB_HEADS = 8
SB_HEAD_DIM = 128
MLA_HEADS = 8
MLA_NOPE = 128
MLA_ROPE = 64
MLA_V = 128
MLA_Q_LORA = 768
MLA_KV_LORA = 512
CONV_WIDTH = 3

LANES = 128
SUBLANES = 8
MLA_QK_PAD = 2 * LANES
VMEM_LIMIT_BYTES = 56 * 1024 * 1024

_SBW = SB_HEADS * SB_HEAD_DIM
_IN_SBQ = 0
_IN_SBK = _SBW
_IN_SBV = 2 * _SBW
_IN_GSB = 3 * _SBW
_IN_BLOCK = 512


def _params(n_axes):
    return pltpu.CompilerParams(
        dimension_semantics=("arbitrary",) * n_axes, vmem_limit_bytes=VMEM_LIMIT_BYTES)


def _rms(x, g):
    return x * lax.rsqrt(jnp.mean(x * x, axis=-1, keepdims=True) + EPS) * g


def _rope_group(g, c, s1, s2):
    return g * c + pltpu.roll(g, 96, 1) * s1 + pltpu.roll(g, 32, 1) * s2


def _dot(a, b):
    return jnp.dot(a, b, preferred_element_type=F32)


def _dot_nt(a, b):
    return lax.dot_general(a, b, (((1,), (1,)), ((), ())), preferred_element_type=F32)


def _rmsnorm_kernel(x_ref, g_ref, o_ref):
    o_ref[...] = _rms(x_ref[...], g_ref[...]).astype(o_ref.dtype)


def _rmsnorm(x, g, bm):
    t, d = x.shape
    return pl.pallas_call(
        _rmsnorm_kernel,
        out_shape=jax.ShapeDtypeStruct((t, d), x.dtype),
        grid=(t // bm,),
        in_specs=[pl.BlockSpec((bm, d), lambda i: (i, 0)),
                  pl.BlockSpec((1, d), lambda i: (0, 0))],
        out_specs=pl.BlockSpec((bm, d), lambda i: (i, 0)),
        compiler_params=_params(1),
        name="rmsnorm",
    )(x, g.reshape(1, d))


def _norm_matmul_kernel(x_ref, g_ref, w_ref, o_ref, h_ref):
    @pl.when(pl.program_id(1) == 0)
    def _():
        h_ref[...] = _rms(x_ref[...], g_ref[...]).astype(h_ref.dtype)

    o_ref[...] = _dot_nt(h_ref[...], w_ref[...]).astype(o_ref.dtype)


def _norm_matmul(x, g, w_t, layer, bm, bn):
    t, k = x.shape
    n = w_t.shape[1]
    return pl.pallas_call(
        _norm_matmul_kernel,
        out_shape=jax.ShapeDtypeStruct((t, n), BF16),
        grid=(t // bm, n // bn),
        in_specs=[pl.BlockSpec((bm, k), lambda i, j: (i, 0)),
                  pl.BlockSpec((1, k), lambda i, j: (0, 0)),
                  pl.BlockSpec((None, bn, k), lambda i, j: (layer, j, 0))],
        out_specs=pl.BlockSpec((bm, bn), lambda i, j: (i, j)),
        scratch_shapes=[pltpu.VMEM((bm, k), BF16)],
        compiler_params=_params(2),
        name="in_proj",
    )(x, g.reshape(1, k), w_t)


def _mla_q_kernel(cq_ref, g_ref, w_ref, c_ref, s1_ref, s2_ref, o_ref):
    h = _rms(cq_ref[...].astype(F32), g_ref[...]).astype(BF16)
    y = _dot(h, w_ref[...])
    c, s1, s2 = c_ref[...], s1_ref[...], s2_ref[...]
    for hd in range(MLA_HEADS):
        lo = hd * MLA_QK_PAD
        o_ref[:, lo:lo + LANES] = y[:, lo:lo + LANES].astype(o_ref.dtype)
        rot = _rope_group(y[:, lo + LANES:lo + MLA_QK_PAD], c, s1, s2)
        o_ref[:, lo + LANES:lo + MLA_QK_PAD] = rot.astype(o_ref.dtype)


def _mla_q(proj, cq_col, g, w, tabs, bm, seq):
    t = proj.shape[0]
    n = w.shape[1]
    nsb = seq // bm
    tab_spec = pl.BlockSpec((bm, LANES), lambda i: (i % nsb, 0))
    return pl.pallas_call(
        _mla_q_kernel,
        out_shape=jax.ShapeDtypeStruct((t, n), BF16),
        grid=(t // bm,),
        in_specs=[pl.BlockSpec((bm, MLA_Q_LORA), lambda i: (i, cq_col // MLA_Q_LORA)),
                  pl.BlockSpec((1, MLA_Q_LORA), lambda i: (0, 0)),
                  pl.BlockSpec((MLA_Q_LORA, n), lambda i: (0, 0)),
                  tab_spec, tab_spec, tab_spec],
        out_specs=pl.BlockSpec((bm, n), lambda i: (i, 0)),
        compiler_params=_params(1),
        name="mla_q",
    )(proj, g.reshape(1, MLA_Q_LORA), w, *tabs)


def _mla_kv_kernel(ckv_ref, kr_ref, g_ref, w_ref, c_ref, s1_ref, s2_ref, k_ref, v_ref):
    h = _rms(ckv_ref[...].astype(F32), g_ref[...]).astype(BF16)
    y = _dot(h, w_ref[...])
    rot = _rope_group(kr_ref[...].astype(F32), c_ref[...], s1_ref[...], s2_ref[...])
    rot = rot.astype(k_ref.dtype)
    kw = MLA_HEADS * MLA_NOPE
    for hd in range(MLA_HEADS):
        lo = hd * MLA_QK_PAD
        k_ref[:, lo:lo + LANES] = y[:, hd * MLA_NOPE:(hd + 1) * MLA_NOPE].astype(k_ref.dtype)
        k_ref[:, lo + LANES:lo + MLA_QK_PAD] = rot
    v_ref[...] = y[:, kw:].astype(v_ref.dtype)


def _mla_kv(proj, ckv_col, kr_col, g, w, tabs, bm, seq):
    t = proj.shape[0]
    nsb = seq // bm
    tab_spec = pl.BlockSpec((bm, LANES), lambda i: (i % nsb, 0))
    return pl.pallas_call(
        _mla_kv_kernel,
        out_shape=(jax.ShapeDtypeStruct((t, MLA_HEADS * MLA_QK_PAD), BF16),
                   jax.ShapeDtypeStruct((t, MLA_HEADS * MLA_V), BF16)),
        grid=(t // bm,),
        in_specs=[pl.BlockSpec((bm, MLA_KV_LORA), lambda i: (i, ckv_col // MLA_KV_LORA)),
                  pl.BlockSpec((bm, LANES), lambda i: (i, kr_col // LANES)),
                  pl.BlockSpec((1, MLA_KV_LORA), lambda i: (0, 0)),
                  pl.BlockSpec(w.shape, lambda i: (0, 0)),
                  tab_spec, tab_spec, tab_spec],
        out_specs=(pl.BlockSpec((bm, MLA_HEADS * MLA_QK_PAD), lambda i: (i, 0)),
                   pl.BlockSpec((bm, MLA_HEADS * MLA_V), lambda i: (i, 0))),
        compiler_params=_params(1),
        name="mla_kv",
    )(proj, proj, g.reshape(1, MLA_KV_LORA), w, *tabs)


def _mla_qkv_kernel(cq_ref, ckv_ref, kr_ref, gq_ref, gkv_ref, wq_ref, wkv_ref,
                    c_ref, s1_ref, s2_ref, q_ref, k_ref, v_ref):
    _mla_q_kernel(cq_ref, gq_ref, wq_ref, c_ref, s1_ref, s2_ref, q_ref)
    _mla_kv_kernel(ckv_ref, kr_ref, gkv_ref, wkv_ref, c_ref, s1_ref, s2_ref, k_ref, v_ref)


def _mla_qkv(proj, cols, gq, gkv, wq, wkv, tabs, bm, seq):
    t = proj.shape[0]
    nsb = seq // bm
    tab_spec = pl.BlockSpec((bm, LANES), lambda i: (i % nsb, 0))
    qk_w, v_w = MLA_HEADS * MLA_QK_PAD, MLA_HEADS * MLA_V
    return pl.pallas_call(
        _mla_qkv_kernel,
        out_shape=(jax.ShapeDtypeStruct((t, qk_w), BF16), jax.ShapeDtypeStruct((t, qk_w), BF16),
                   jax.ShapeDtypeStruct((t, v_w), BF16)),
        grid=(t // bm,),
        in_specs=[pl.BlockSpec((bm, MLA_Q_LORA), lambda i: (i, cols["cq"] // MLA_Q_LORA)),
                  pl.BlockSpec((bm, MLA_KV_LORA), lambda i: (i, cols["ckv"] // MLA_KV_LORA)),
                  pl.BlockSpec((bm, LANES), lambda i: (i, cols["krope"] // LANES)),
                  pl.BlockSpec((1, MLA_Q_LORA), lambda i: (0, 0)),
                  pl.BlockSpec((1, MLA_KV_LORA), lambda i: (0, 0)),
                  pl.BlockSpec(wq.shape, lambda i: (0, 0)),
                  pl.BlockSpec(wkv.shape, lambda i: (0, 0)),
                  tab_spec, tab_spec, tab_spec],
        out_specs=(pl.BlockSpec((bm, qk_w), lambda i: (i, 0)),
                   pl.BlockSpec((bm, qk_w), lambda i: (i, 0)),
                   pl.BlockSpec((bm, v_w), lambda i: (i, 0))),
        compiler_params=_params(1),
        name="mla_qkv",
    )(proj, proj, proj, gq.reshape(1, MLA_Q_LORA), gkv.reshape(1, MLA_KV_LORA), wq, wkv, *tabs)


def _sb_attn_kernel(q_ref, k_ref, v_ref, o_ref, run_ref, acc_ref, *, bq, bk):
    i = pl.program_id(2)
    n_diag = bq // bk
    row = lax.broadcasted_iota(jnp.int32, (bk, bk), 0)
    col = lax.broadcasted_iota(jnp.int32, (bk, bk), 1)
    later = (row > col).astype(BF16)
    later2 = jnp.concatenate([later, later], axis=0)

    run_ref[...] = jnp.zeros(run_ref.shape, F32)
    acc_ref[...] = jnp.zeros(acc_ref.shape, F32)

    def tile(kb, diag):
        rows = slice(0 if diag is None else diag * bk, bq)
        start = pl.multiple_of(kb * bk, bk)
        k = k_ref[pl.ds(start, bk), :]
        v = v_ref[pl.ds(start, bk), :]
        z = _dot_nt(q_ref[rows, :], k)
        z_pos = jnp.maximum(z, 0.0)
        z_neg = z - z_pos
        log_term = jnp.log(1.0 + jnp.exp2(z_neg - z_pos)) * LOG2E
        sp = z_pos + log_term
        if diag is not None:
            causal = (lax.broadcasted_iota(jnp.int32, z.shape, 1)
                      < lax.broadcasted_iota(jnp.int32, z.shape, 0))
            sp = jnp.where(causal, sp, 0.0)
        hi = sp.astype(BF16)
        split = jnp.concatenate([hi, (sp - hi.astype(F32)).astype(BF16)], axis=1)
        suffix = _dot(split, later2)
        a = jnp.exp2(z_neg - log_term - suffix)
        if diag is not None:
            a = jnp.where(causal, a, 0.0)
        run = run_ref[rows, :]
        acc_ref[rows, :] += jnp.exp2(-run) * _dot(a.astype(BF16), v)
        run_ref[rows, :] = run + (suffix[:, 0:1] + sp[:, 0:1])

    for d in range(n_diag - 1, -1, -1):
        tile(i * n_diag + d, d)
    n_before = i * n_diag

    def body(t, carry):
        tile(n_before - 1 - t, None)
        return carry

    lax.fori_loop(0, n_before, body, 0)
    o_ref[...] = acc_ref[...].astype(o_ref.dtype)


def _sb_attn(proj, batch, seq, bq, bk):
    t = proj.shape[0]
    nq = seq // bq
    qc, kc, vc = (c // SB_HEAD_DIM for c in (_IN_SBQ, _IN_SBK, _IN_SBV))
    return pl.pallas_call(
        functools.partial(_sb_attn_kernel, bq=bq, bk=bk),
        out_shape=jax.ShapeDtypeStruct((t, _SBW), BF16),
        grid=(batch, SB_HEADS, nq),
        in_specs=[pl.BlockSpec((bq, SB_HEAD_DIM), lambda b, h, i: (b * nq + i, qc + h)),
                  pl.BlockSpec((seq, SB_HEAD_DIM), lambda b, h, i: (b, kc + h)),
                  pl.BlockSpec((seq, SB_HEAD_DIM), lambda b, h, i: (b, vc + h))],
        out_specs=pl.BlockSpec((bq, SB_HEAD_DIM), lambda b, h, i: (b * nq + i, h)),
        scratch_shapes=[pltpu.VMEM((bq, LANES), F32), pltpu.VMEM((bq, SB_HEAD_DIM), F32)],
        compiler_params=_params(3),
        name="sb_attn",
    )(proj, proj, proj)


def _mla_attn_kernel(q_ref, k_ref, v_ref, o_ref, m_ref, acc_ref, *, bq, bk, sub):
    i = pl.program_id(2)
    n_diag = bq // bk
    m_ref[...] = jnp.full(m_ref.shape, -jnp.inf, F32)
    acc_ref[...] = jnp.zeros(acc_ref.shape, F32)
    ones = jnp.ones((bk, LANES), BF16)

    def tile(kb, diag):
        start = pl.multiple_of(kb * bk, bk)
        k = k_ref[pl.ds(start, bk), :]
        v1 = jnp.concatenate([v_ref[pl.ds(start, bk), :], ones], axis=1)
        for s in range(bq // sub):
            rows = slice(s * sub, (s + 1) * sub)
            if diag is not None and (s + 1) * sub <= diag * bk:
                continue
            sc = _dot_nt(q_ref[rows, :], k)
            if diag is not None and s * sub < (diag + 1) * bk:
                qchunk = (lax.broadcasted_iota(jnp.int32, (sub, bk), 0) + s * sub) // CHUNK
                kchunk = (lax.broadcasted_iota(jnp.int32, (sub, bk), 1) + diag * bk) // CHUNK
                sc = jnp.where(kchunk <= qchunk, sc, -jnp.inf)
            m = m_ref[rows, :]
            m_new = jnp.maximum(m, jnp.max(sc, axis=1, keepdims=True))
            alpha = jnp.exp2(m - m_new)
            p = jnp.exp2(sc - jnp.tile(m_new, (1, bk // LANES)))
            acc_ref[rows, :] = (jnp.tile(alpha, (1, 2)) * acc_ref[rows, :]
                                + _dot(p.astype(BF16), v1))
            m_ref[rows, :] = m_new

    for d in range(n_diag):
        tile(i * n_diag + d, d)
    n_before = i * n_diag

    def body(t, carry):
        tile(n_before - 1 - t, None)
        return carry

    lax.fori_loop(0, n_before, body, 0)
    o_ref[...] = (acc_ref[:, 0:MLA_V] / acc_ref[:, MLA_V:2 * MLA_V]).astype(o_ref.dtype)


def _mla_attn(q, k, v, batch, seq, bq, bk, sub):
    t = q.shape[0]
    nq = seq // bq
    return pl.pallas_call(
        functools.partial(_mla_attn_kernel, bq=bq, bk=bk, sub=sub),
        out_shape=jax.ShapeDtypeStruct((t, MLA_HEADS * MLA_V), BF16),
        grid=(batch, MLA_HEADS, nq),
        in_specs=[pl.BlockSpec((bq, MLA_QK_PAD), lambda b, h, i: (b * nq + i, h)),
                  pl.BlockSpec((seq, MLA_QK_PAD), lambda b, h, i: (b, h)),
                  pl.BlockSpec((seq, MLA_V), lambda b, h, i: (b, h))],
        out_specs=pl.BlockSpec((bq, MLA_V), lambda b, h, i: (b * nq + i, h)),
        scratch_shapes=[pltpu.VMEM((bq, LANES), F32), pltpu.VMEM((bq, 2 * MLA_V), F32)],
        compiler_params=_params(3),
        name="mla_attn",
    )(q, k, v)


def _merge_kernel(osb_ref, omla_ref, wsb_ref, wmla_ref, gsb_ref, gmla_ref, bsb_ref, bmla_ref,
                  o_ref):
    gate_sb = jax.nn.sigmoid(gsb_ref[...].astype(F32) + bsb_ref[...])
    gate_mla = jax.nn.sigmoid(gmla_ref[...].astype(F32) + bmla_ref[...])
    mixed = (gate_sb * _dot(osb_ref[...], wsb_ref[...])
             + gate_mla * _dot(omla_ref[...], wmla_ref[...]))
    o_ref[...] = mixed.astype(o_ref.dtype)


def _merge(o_sb, o_mla, w_sb, w_mla, layer, proj, gsb_col, gmla_col, b_gate, bm, bn):
    t, k = o_sb.shape
    d = w_sb.shape[2]
    nj = d // bn
    return pl.pallas_call(
        _merge_kernel,
        out_shape=jax.ShapeDtypeStruct((t, d), BF16),
        grid=(t // bm, nj),
        in_specs=[pl.BlockSpec((bm, k), lambda i, j: (i, 0)),
                  pl.BlockSpec((bm, k), lambda i, j: (i, 0)),
                  pl.BlockSpec((None, k, bn), lambda i, j: (layer, 0, j)),
                  pl.BlockSpec((None, k, bn), lambda i, j: (layer, 0, j)),
                  pl.BlockSpec((bm, bn), lambda i, j: (i, gsb_col // bn + j)),
                  pl.BlockSpec((bm, bn), lambda i, j: (i, gmla_col // bn + j)),
                  pl.BlockSpec((1, bn), lambda i, j: (0, j)),
                  pl.BlockSpec((1, bn), lambda i, j: (0, nj + j))],
        out_specs=pl.BlockSpec((bm, bn), lambda i, j: (i, j)),
        compiler_params=_params(2),
        name="merge",
    )(o_sb, o_mla, w_sb, w_mla, proj, proj, b_gate, b_gate)


def _residual_norm_kernel(a_ref, w_ref, x_ref, g_ref, o_ref, h_ref):
    out = x_ref[...] + _dot(a_ref[...], w_ref[...])
    o_ref[...] = out
    h_ref[...] = _rms(out, g_ref[...]).astype(h_ref.dtype)


def _residual_norm(a, w, layer, x, g, bm):
    t, k = a.shape
    d = w.shape[2]
    return pl.pallas_call(
        _residual_norm_kernel,
        out_shape=(jax.ShapeDtypeStruct((t, d), F32), jax.ShapeDtypeStruct((t, d), BF16)),
        grid=(t // bm,),
        in_specs=[pl.BlockSpec((bm, k), lambda i: (i, 0)),
                  pl.BlockSpec((None, k, d), lambda i: (layer, 0, 0)),
                  pl.BlockSpec((bm, d), lambda i: (i, 0)),
                  pl.BlockSpec((1, d), lambda i: (0, 0))],
        out_specs=(pl.BlockSpec((bm, d), lambda i: (i, 0)),
                   pl.BlockSpec((bm, d), lambda i: (i, 0))),
        compiler_params=_params(1),
        name="residual_norm",
    )(a, w, x, g.reshape(1, d))


def _residual_kernel(a_ref, w_ref, x_ref, o_ref):
    o_ref[...] = x_ref[...] + _dot(a_ref[...], w_ref[...])


def _residual(a, w, layer, x, bm, bn):
    t, k = a.shape
    d = w.shape[2]
    return pl.pallas_call(
        _residual_kernel,
        out_shape=jax.ShapeDtypeStruct((t, d), F32),
        grid=(d // bn, t // bm),
        in_specs=[pl.BlockSpec((bm, k), lambda j, i: (i, 0)),
                  pl.BlockSpec((None, k, bn), lambda j, i: (layer, 0, j)),
                  pl.BlockSpec((bm, bn), lambda j, i: (i, j))],
        out_specs=pl.BlockSpec((bm, bn), lambda j, i: (i, j)),
        compiler_params=_params(2),
        name="ffn_down",
    )(a, w, x)


def _ffn_up_kernel(h_ref, wg_ref, wv_ref, cwg_ref, cwv_ref, cbg_ref, cbv_ref, o_ref,
                   sg_ref, sv_ref, wgb_ref, wvb_ref, *, bm, chunk, blocks_per_seq):
    i = pl.program_id(1)

    @pl.when(i == 0)
    def _():
        wgb_ref[...] = wg_ref[...].astype(wgb_ref.dtype)
        wvb_ref[...] = wv_ref[...].astype(wvb_ref.dtype)

    starts_sequence = (i % blocks_per_seq) == 0
    for s_ref in (sg_ref, sv_ref):
        s_ref[0:SUBLANES, :] = jnp.where(starts_sequence, 0.0, s_ref[bm:bm + SUBLANES, :])

    def conv(s_ref, cw_ref, cb_ref, cols):
        lo = SUBLANES
        acc = cb_ref[:, cols] + cw_ref[0:1, cols] * s_ref[lo - 2:lo - 2 + bm, cols]
        acc = acc + cw_ref[1:2, cols] * s_ref[lo - 1:lo - 1 + bm, cols]
        return acc + cw_ref[2:3, cols] * s_ref[lo:lo + bm, cols]

    h = h_ref[...]
    chunks = [slice(c * chunk, (c + 1) * chunk) for c in range(o_ref.shape[1] // chunk)]
    for cols in chunks:
        sg_ref[SUBLANES:SUBLANES + bm, cols] = _dot(h, wgb_ref[:, cols])
    sv_ref[SUBLANES:SUBLANES + bm, chunks[0]] = _dot(h, wvb_ref[:, chunks[0]])
    gates = []
    for cols in chunks:
        gate = conv(sg_ref, cwg_ref, cbg_ref, cols)
        gates.append(gate * jax.nn.sigmoid(gate))
    for cols in chunks[1:]:
        sv_ref[SUBLANES:SUBLANES + bm, cols] = _dot(h, wvb_ref[:, cols])
    for cols, gate in zip(chunks, gates):
        val = conv(sv_ref, cwv_ref, cbv_ref, cols)
        o_ref[:, cols] = (gate * val).astype(o_ref.dtype)


def _ffn_up(h, w_up, layer, conv_w, conv_b, seq, bm, bn, chunk):
    t, d = h.shape
    d_ff = w_up.shape[2] // 2
    nj = d_ff // bn
    return pl.pallas_call(
        functools.partial(_ffn_up_kernel, bm=bm, chunk=chunk, blocks_per_seq=seq // bm),
        out_shape=jax.ShapeDtypeStruct((t, d_ff), BF16),
        grid=(nj, t // bm),
        in_specs=[pl.BlockSpec((bm, d), lambda j, i: (i, 0)),
                  pl.BlockSpec((None, d, bn), lambda j, i: (layer, 0, j)),
                  pl.BlockSpec((None, d, bn), lambda j, i: (layer, 0, nj + j)),
                  pl.BlockSpec((CONV_WIDTH, bn), lambda j, i: (0, j)),
                  pl.BlockSpec((CONV_WIDTH, bn), lambda j, i: (0, nj + j)),
                  pl.BlockSpec((1, bn), lambda j, i: (0, j)),
                  pl.BlockSpec((1, bn), lambda j, i: (0, nj + j))],
        out_specs=pl.BlockSpec((bm, bn), lambda j, i: (i, j)),
        scratch_shapes=[pltpu.VMEM((bm + SUBLANES, bn), F32),
                        pltpu.VMEM((bm + SUBLANES, bn), F32),
                        pltpu.VMEM((d, bn), BF16), pltpu.VMEM((d, bn), BF16)],
        compiler_params=_params(2),
        name="ffn_up",
    )(h, w_up, w_up, conv_w, conv_w, conv_b, conv_b)


def _rope_tables(seq):
    half = MLA_ROPE // 2
    inv = 1.0 / (ROPE_THETA ** (jnp.arange(0, MLA_ROPE, 2, dtype=F32) / MLA_ROPE))
    ang = jnp.arange(seq, dtype=F32)[:, None] * inv[None, :]
    cos, sin = jnp.cos(ang), jnp.sin(ang)
    zero = jnp.zeros((seq, half), F32)
    tail = jnp.zeros((seq, LANES - MLA_ROPE), F32)
    c = jnp.concatenate([cos, cos, tail], axis=1)
    s1 = jnp.concatenate([-sin, zero, tail], axis=1)
    s2 = jnp.concatenate([zero, sin, tail], axis=1)
    return c, s1, s2


def _block(n, want):
    b = min(n, want)
    assert n % b == 0, (n, want)
    return b


def _in_layout(d_model):
    src = {"sbq": 0, "sbk": _SBW, "sbv": 2 * _SBW, "cq": 3 * _SBW}
    src["ckv"] = src["cq"] + MLA_Q_LORA
    src["krope"] = src["ckv"] + MLA_KV_LORA
    src["gsb"] = src["krope"] + MLA_ROPE
    src["gmla"] = src["gsb"] + d_model
    src["end"] = src["gmla"] + d_model
    dst = {"sbq": _IN_SBQ, "sbk": _IN_SBK, "sbv": _IN_SBV, "gsb": _IN_GSB,
           "gmla": _IN_GSB + d_model}
    dst["ckv"] = dst["gmla"] + d_model
    dst["cq"] = dst["ckv"] + MLA_KV_LORA
    dst["krope"] = dst["cq"] + MLA_Q_LORA
    dst["end"] = -(-(dst["krope"] + LANES) // _IN_BLOCK) * _IN_BLOCK
    assert dst["ckv"] % MLA_KV_LORA == 0 and dst["cq"] % MLA_Q_LORA == 0
    assert dst["krope"] % LANES == 0 and dst["gmla"] % _IN_BLOCK == 0
    return src, dst


def _prep_w_in_starts(d_model):
    src, dst = _in_layout(d_model)
    widths = {"sbq": _SBW, "sbk": _SBW, "sbv": _SBW, "gsb": d_model, "gmla": d_model,
              "ckv": MLA_KV_LORA, "cq": MLA_Q_LORA}
    starts = []
    for b in range(dst["end"] // _IN_BLOCK):
        r = b * _IN_BLOCK
        name = max((n for n in widths if dst[n] <= r), key=lambda n: dst[n])
        assert r + _IN_BLOCK <= dst[name] + widths[name] or b == dst["end"] // _IN_BLOCK - 1
        starts.append(src[name] + r - dst[name])
    return tuple(starts)


def _prep_w_in_kernel(a_ref, kr_ref, o_ref, *, d_model):
    _, dst = _in_layout(d_model)
    b = pl.program_id(1)
    n_blocks = dst["end"] // _IN_BLOCK
    n_scaled = _SBW // _IN_BLOCK
    cq_rows = dst["krope"] - (n_blocks - 1) * _IN_BLOCK

    @pl.when(b < n_scaled)
    def _():
        o_ref[...] = (a_ref[0] * (SB_HEAD_DIM ** -0.5 * LOG2E)).astype(o_ref.dtype)

    @pl.when((b >= n_scaled) & (b < n_blocks - 1))
    def _():
        o_ref[...] = a_ref[0].astype(o_ref.dtype)

    @pl.when(b == n_blocks - 1)
    def _():
        o_ref[0:cq_rows, :] = a_ref[0, 0:cq_rows, :].astype(o_ref.dtype)
        o_ref[cq_rows:cq_rows + MLA_ROPE, :] = kr_ref[0].astype(o_ref.dtype)
        o_ref[cq_rows + MLA_ROPE:, :] = jnp.zeros(
            (_IN_BLOCK - cq_rows - MLA_ROPE, o_ref.shape[1]), o_ref.dtype)


def _prep_w_in(w_in):
    depth, k, n = w_in.shape
    src, dst = _in_layout(k)
    assert src["end"] == n
    starts = _prep_w_in_starts(k)
    w_t = jnp.swapaxes(w_in, 1, 2)

    def a_map(l, b):
        start = jnp.int32(starts[-1])
        for blk in range(len(starts) - 2, -1, -1):
            start = jnp.where(b == blk, starts[blk], start)
        return (l, pl.multiple_of(start, MLA_ROPE), 0)

    return pl.pallas_call(
        functools.partial(_prep_w_in_kernel, d_model=k),
        out_shape=jax.ShapeDtypeStruct((depth, dst["end"], k), BF16),
        grid=(depth, len(starts)),
        in_specs=[pl.BlockSpec((pl.Element(1), pl.Element(_IN_BLOCK), pl.Element(k)), a_map),
                  pl.BlockSpec((pl.Element(1), pl.Element(MLA_ROPE), pl.Element(k)),
                               lambda l, b: (l, src["krope"], 0))],
        out_specs=pl.BlockSpec((None, _IN_BLOCK, k), lambda l, b: (l, b, 0)),
        compiler_params=_params(2),
        name="prep_w_in",
    )(w_t, w_t)


def _cast_kernel(w_ref, o_ref):
    o_ref[...] = w_ref[...].astype(o_ref.dtype)


def _cast_bf16(w, rows):
    depth, k, n = w.shape
    return pl.pallas_call(
        _cast_kernel,
        out_shape=jax.ShapeDtypeStruct(w.shape, BF16),
        grid=(depth, k // rows),
        in_specs=[pl.BlockSpec((None, rows, n), lambda l, i: (l, i, 0))],
        out_specs=pl.BlockSpec((None, rows, n), lambda l, i: (l, i, 0)),
        compiler_params=_params(2),
        name="cast_bf16",
    )(w)


def _prep_w_uq(w_uq):
    per = MLA_NOPE + MLA_ROPE
    w = w_uq.reshape(w_uq.shape[0], MLA_HEADS, per) * (per ** -0.5 * LOG2E)
    w = jnp.pad(w, ((0, 0), (0, 0), (0, MLA_QK_PAD - per)))
    return w.reshape(w_uq.shape[0], MLA_HEADS * MLA_QK_PAD).astype(BF16)


def _prep_w_ukv(w_ukv):
    w = w_ukv.reshape(w_ukv.shape[0], MLA_HEADS, MLA_NOPE + MLA_V)
    k = w[:, :, :MLA_NOPE].reshape(w_ukv.shape[0], MLA_HEADS * MLA_NOPE)
    v = w[:, :, MLA_NOPE:].reshape(w_ukv.shape[0], MLA_HEADS * MLA_V)
    return jnp.concatenate([k, v], axis=1).astype(BF16)


def kernel(x, norm1_g, w_in, b_gate, q_norm_g, w_uq, kv_norm_g, w_ukv, w_proj_sb, w_proj_mla,
           w_out, norm2_g, w_up, conv_w, conv_b, w_down, final_g):
    batch, seq, d_model = x.shape
    depth = w_in.shape[0]
    t = batch * seq
    tabs = _rope_tables(seq)

    bm_big = _block(seq, 1024)
    bm_mid = _block(seq, 512)
    sb_bq, sb_bk = _block(seq, 4096), _block(seq, 256)
    mla_bq = _block(seq, 4096)
    mla_bk, mla_sub = _block(mla_bq, 512), _block(mla_bq, 256)

    _, cols = _in_layout(d_model)
    w_in_b = _prep_w_in(w_in)
    w_sb_b, w_mla_b = _cast_bf16(w_proj_sb, 512), _cast_bf16(w_proj_mla, 512)
    w_out_b, w_down_b = _cast_bf16(w_out, 512), _cast_bf16(w_down, 512)

    xf = x.reshape(t, d_model)
    for l in range(depth):
        proj = _norm_matmul(xf, norm1_g[l], w_in_b, l, bm_big, _IN_BLOCK)
        q_mla, k_mla, v_mla = _mla_qkv(proj, cols, q_norm_g[l], kv_norm_g[l],
                                       _prep_w_uq(w_uq[l]), _prep_w_ukv(w_ukv[l]), tabs,
                                       bm_mid, seq)
        o_sb = _sb_attn(proj, batch, seq, sb_bq, sb_bk)
        o_mla = _mla_attn(q_mla, k_mla, v_mla, batch, seq, mla_bq, mla_bk, mla_sub)
        mixed = _merge(o_sb, o_mla, w_sb_b, w_mla_b, l, proj, cols["gsb"], cols["gmla"],
                       b_gate[l].reshape(1, -1), bm_big, 2 * _IN_BLOCK)
        x1, h2 = _residual_norm(mixed, w_out_b, l, xf, norm2_g[l], bm_mid)
        act = _ffn_up(h2, w_up, l, conv_w[l], conv_b[l].reshape(1, -1), seq,
                      bm_big, 512, 256)
        xf = _residual(act, w_down_b, l, x1, bm_mid, 1024)
    return _rmsnorm(xf, final_g, bm_mid).reshape(batch, seq, d_model)
```
